```python
import jax, jax.numpy as jnp
from jax import lax
import numpy as np

D_MODEL = 1024
BATCH = 4
SEQ = 4096
DEPTH = 4
DEC_BATCH = 32
DEC_SEQ = 8
PAST_LEN = 8192
PAGE_SIZE = 128

HEAD_DIM = 64
D_ATT = D_MODEL // 2
N_HEADS = D_ATT // HEAD_DIM
N_IDX_HEADS = 8
IDX_DIM = 64
TOPK_MAX = 256
QBLOCK = 128
D_POOL = D_MODEL // 2
POOL_WINDOWS = (2, 4, 8, 16)
N_POOL_GROUPS = len(POOL_WINDOWS)
POOL_GROUP_DIM = D_POOL // N_POOL_GROUPS
POOL_HIST = max(POOL_WINDOWS) - 1
N_BUCKETS = 32
MAX_DISTANCE = 128
RMS_EPS = 1e-6
IN_SIZES = (D_ATT, D_ATT, D_ATT, D_ATT, N_IDX_HEADS * IDX_DIM, IDX_DIM, N_IDX_HEADS,
            D_POOL, D_POOL, D_MODEL, D_MODEL)
D_IN = sum(IN_SIZES)

kernel_name = "dsa_pool_gated_hybrid_step"


def _rmsnorm(x, g):
    x32 = x.astype(jnp.float32)
    y = x32 * lax.rsqrt(jnp.mean(x32 * x32, axis=-1, keepdims=True) + RMS_EPS)
    return (y * g.astype(jnp.float32)).astype(x.dtype)


def _in_proj(h, w_in_l):
    B, T = h.shape[:2]
    p = jnp.einsum('btd,de->bte', h, w_in_l)
    (q, k, v, z_a, q_idx, k_idx, w_idx, u_p, z_p, g_a, g_p) = jnp.split(
        p, np.cumsum(IN_SIZES)[:-1].tolist(), axis=-1)
    q = q.reshape(B, T, N_HEADS, HEAD_DIM)
    k = k.reshape(B, T, N_HEADS, HEAD_DIM)
    v = v.reshape(B, T, N_HEADS, HEAD_DIM)
    q_idx = q_idx.reshape(B, T, N_IDX_HEADS, IDX_DIM)
    return q, k, v, z_a, q_idx, k_idx, w_idx, u_p, z_p, g_a, g_p


def _t5_bucket(dist):
    n = jnp.maximum(dist, 0)
    max_exact = N_BUCKETS // 2
    nf = jnp.maximum(n, 1).astype(jnp.float32)
    large = max_exact + (jnp.log(nf / max_exact) / np.log(MAX_DISTANCE / max_exact)
                         * (N_BUCKETS - max_exact)).astype(jnp.int32)
    large = jnp.minimum(large, N_BUCKETS - 1)
    return jnp.where(n < max_exact, n, large)


def _sparse_attend(q, q_idx, w_idx, k_idx_all, q_pos, gather_kv, topk, rel_bias):
    L = k_idx_all.shape[1]
    logits = jax.nn.relu(jnp.einsum('bthd,bsd->bths', q_idx.astype(jnp.float32),
                                    k_idx_all.astype(jnp.float32)) * IDX_DIM ** -0.5)
    score = jnp.einsum('bth,bths->bts', w_idx.astype(jnp.float32) * N_IDX_HEADS ** -0.5, logits)
    key_pos = jnp.arange(L, dtype=jnp.int32)
    score = jnp.where(key_pos[None, None, :] <= q_pos[None, :, None], score, -jnp.inf)
    _, sel = lax.top_k(score, topk)
    kg, vg = gather_kv(sel)
    dist = q_pos[None, :, None] - sel
    bias = rel_bias[_t5_bucket(dist)].astype(jnp.float32)
    s = (jnp.einsum('bthd,btkhd->bthk', q, kg).astype(jnp.float32) * HEAD_DIM ** -0.5
         + jnp.moveaxis(bias, -1, 2))
    s = jnp.where((dist >= 0)[:, :, None, :], s, -jnp.inf)
    p = jax.nn.softmax(s, axis=-1).astype(vg.dtype)
    return jnp.einsum('bthk,btkhd->bthd', p, vg)


def _attn_prompt(q, k, v, q_idx, k_idx, w_idx, rel_bias):
    B, S = q.shape[:2]
    nb = S // QBLOCK
    topk = min(TOPK_MAX, S // 4)
    take = jax.vmap(lambda rows, ix: rows[ix])

    def to_blocks(a):
        return jnp.moveaxis(a.reshape((B, nb, QBLOCK) + a.shape[2:]), 1, 0)

    def gather_kv(sel):
        return take(k, sel), take(v, sel)

    def one_block(args):
        q_b, qi_b, w_b, pos_b = args
        return _sparse_attend(q_b, qi_b, w_b, k_idx, pos_b, gather_kv, topk, rel_bias)

    pos = jnp.arange(S, dtype=jnp.int32).reshape(nb, QBLOCK)
    out = lax.map(one_block, (to_blocks(q), to_blocks(q_idx), to_blocks(w_idx), pos))
    return jnp.moveaxis(out, 0, 1).reshape(B, S, D_ATT)


def _attn_sample(q, k_new, v_new, q_idx, k_idx_new, w_idx, cache_k_l, cache_v_l,
                 cache_idx_l, page_table, rel_bias):
    DB, T = q.shape[:2]
    past = PAST_LEN
    k_idx_past = cache_idx_l[page_table].reshape(DB, past, IDX_DIM)
    k_idx_all = jnp.concatenate([k_idx_past, k_idx_new.astype(k_idx_past.dtype)], axis=1)
    q_pos = past + jnp.arange(T, dtype=jnp.int32)
    topk = min(TOPK_MAX, (past + T) // 4)
    take = jax.vmap(lambda rows, ix: rows[ix])

    def gather_kv(sel):
        is_new = (sel >= past)[..., None, None]
        s_past = jnp.minimum(sel, past - 1)
        phys = take(page_table, s_past // PAGE_SIZE)
        slot = s_past % PAGE_SIZE
        j = jnp.clip(sel - past, 0, T - 1)
        kg = jnp.where(is_new, take(k_new, j), cache_k_l[phys, slot])
        vg = jnp.where(is_new, take(v_new, j), cache_v_l[phys, slot])
        return kg, vg

    out = _sparse_attend(q, q_idx, w_idx, k_idx_all, q_pos, gather_kv, topk, rel_bias)
    return out.reshape(DB, T, D_ATT)


def _pool_mix(u_ext, n_prev, w_group_l, pool_scale_l):
    B, L_tot = u_ext.shape[:2]
    T = L_tot - n_prev
    c = jnp.cumsum(u_ext.astype(jnp.float32), axis=1)
    cpad = jnp.concatenate([jnp.zeros_like(c[:, :1]), c], axis=1)
    t = n_prev + jnp.arange(T)
    hi = cpad[:, t + 1]
    means = []
    for g, w in enumerate(POOL_WINDOWS):
        lo = jnp.maximum(t + 1 - w, 0)
        cnt = (t + 1 - lo).astype(jnp.float32)
        sl = slice(g * POOL_GROUP_DIM, (g + 1) * POOL_GROUP_DIM)
        means.append((hi[..., sl] - cpad[:, lo, sl]) / cnt[None, :, None])
    pooled = jnp.concatenate(means, axis=-1)
    diff = (pooled - u_ext[:, n_prev:].astype(jnp.float32)).astype(u_ext.dtype)
    diff = diff.reshape(B, T, N_POOL_GROUPS, POOL_GROUP_DIM)
    mixed = jnp.einsum('btgc,gcd->btgd', diff, w_group_l).reshape(B, T, D_POOL)
    return mixed * pool_scale_l


def _branch_merge(x, a_out, z_a, p_out, z_p, g_a, g_p, b_gate_l, w_up_a_l, w_up_p_l, w_out_l):
    a = jnp.einsum('bte,ed->btd', a_out * jax.nn.silu(z_a), w_up_a_l)
    p = jnp.einsum('bte,ed->btd', p_out * jax.nn.silu(z_p), w_up_p_l)
    ga = jax.nn.sigmoid(g_a + b_gate_l[0])
    gp = jax.nn.sigmoid(g_p + b_gate_l[1])
    return x + jnp.einsum('btd,de->bte', ga * a + gp * p, w_out_l)


def setup_inputs(seed: int = 0) -> dict:
    key = jax.random.key(seed)
    ks = jax.random.split(key, 20)
    n_pages = PAST_LEN // PAGE_SIZE
    n_pool = (DEC_BATCH * n_pages * 5) // 4
    perm = jax.random.permutation(ks[0], n_pool)
    page_table = perm[:DEC_BATCH * n_pages].reshape(DEC_BATCH, n_pages).astype(jnp.int32)
    nrm = jax.random.normal
    f32 = jnp.float32
    return {
        "x_prompt": nrm(ks[1], (BATCH, SEQ, D_MODEL), f32),
        "x_sample": nrm(ks[2], (DEC_BATCH, DEC_SEQ, D_MODEL), f32),
        "cache_k": nrm(ks[3], (DEPTH, n_pool, PAGE_SIZE, N_HEADS, HEAD_DIM), f32),
        "cache_v": nrm(ks[4], (DEPTH, n_pool, PAGE_SIZE, N_HEADS, HEAD_DIM), f32),
        "cache_idx": nrm(ks[5], (DEPTH, n_pool, PAGE_SIZE, IDX_DIM), f32),
        "state_pool": nrm(ks[6], (DEPTH, DEC_BATCH, POOL_HIST, D_POOL), f32),
        "page_table": page_table,
        "rel_bias": 0.5 * nrm(ks[7], (N_BUCKETS, N_HEADS), f32),
        "norm_g": 1.0 + 0.02 * nrm(ks[8], (DEPTH, D_MODEL), f32),
        "w_in": nrm(ks[9], (DEPTH, D_MODEL, D_IN), f32) * D_MODEL ** -0.5,
        "b_gate": 0.02 * nrm(ks[10], (DEPTH, 2, D_MODEL), f32),
        "w_pool_group": nrm(ks[11], (DEPTH, N_POOL_GROUPS, POOL_GROUP_DIM, POOL_GROUP_DIM), f32) * POOL_GROUP_DIM ** -0.5,
        "pool_scale": 1.0 + 0.02 * nrm(ks[12], (DEPTH, D_POOL), f32),
        "w_up_attn": nrm(ks[13], (DEPTH, D_ATT, D_MODEL), f32) * D_ATT ** -0.5,
        "w_up_pool": nrm(ks[14], (DEPTH, D_POOL, D_MODEL), f32) * D_POOL ** -0.5,
        "w_out": nrm(ks[15], (DEPTH, D_MODEL, D_MODEL), f32) * D_MODEL ** -0.5,
        "final_norm_g": 1.0 + 0.02 * nrm(ks[16], (D_MODEL,), f32),
    }


def reference(x_prompt, x_sample, cache_k, cache_v, cache_idx, state_pool, page_table,
              rel_bias, norm_g, w_in, b_gate, w_pool_group, pool_scale, w_up_attn,
              w_up_pool, w_out, final_norm_g):
    xp, xs = x_prompt, x_sample
    kp, vp, ip, pp, ks_, vs_, is_, ps_ = [], [], [], [], [], [], [], []
    for l in range(DEPTH):
        h = _rmsnorm(xp, norm_g[l])
        q, k, v, z_a, q_idx, k_idx, w_idx, u_p, z_p, g_a, g_p = _in_proj(h, w_in[l])
        a_out = _attn_prompt(q, k, v, q_idx, k_idx, w_idx, rel_bias)
        p_out = _pool_mix(u_p, 0, w_pool_group[l], pool_scale[l])
        xp = _branch_merge(xp, a_out, z_a, p_out, z_p, g_a, g_p, b_gate[l],
                           w_up_attn[l], w_up_pool[l], w_out[l])
        kp.append(k)
        vp.append(v)
        ip.append(k_idx)
        pp.append(u_p[:, -POOL_HIST:])
        h = _rmsnorm(xs, norm_g[l])
        q, k, v, z_a, q_idx, k_idx, w_idx, u_p, z_p, g_a, g_p = _in_proj(h, w_in[l])
        a_out = _attn_sample(q, k, v, q_idx, k_idx, w_idx, cache_k[l], cache_v[l],
                             cache_idx[l], page_table, rel_bias)
        u_ext = jnp.concatenate([state_pool[l].astype(u_p.dtype), u_p], axis=1)
        p_out = _pool_mix(u_ext, POOL_HIST, w_pool_group[l], pool_scale[l])
        xs = _branch_merge(xs, a_out, z_a, p_out, z_p, g_a, g_p, b_gate[l],
                           w_up_attn[l], w_up_pool[l], w_out[l])
        ks_.append(k)
        vs_.append(v)
        is_.append(k_idx)
        ps_.append(u_ext[:, -POOL_HIST:])
    y_prompt = _rmsnorm(xp, final_norm_g)
    y_sample = _rmsnorm(xs, final_norm_g)
    return (y_prompt, y_sample, jnp.stack(kp), jnp.stack(vp), jnp.stack(ip), jnp.stack(pp),
            jnp.stack(ks_), jnp.stack(vs_), jnp.stack(is_), jnp.stack(ps_))
```

```python
import functools

import numpy as np
import jax
import jax.numpy as jnp
from jax import lax
from jax.experimental import pallas as pl
from jax.experimental.pallas import tpu as pltpu

F32 = jnp.float32
BF16 = jnp.bfloat16
I32 = jnp.int32

N_HEADS = 8
HEAD_DIM = 64
N_IDX_HEADS = 8
IDX_DIM = 64
TOPK_MAX = 256
PAGE = 128
POOL_WINDOWS = (2, 4, 8, 16)
POOL_GROUP = 128
POOL_HIST = 15
N_BUCKETS = 32
MAX_DISTANCE = 128
RMS_EPS = 1e-6

LANES = 128
SUBLANES = 8
VMEM_LIMIT = 56 * 1024 * 1024

TQ = 256
CH = 128
SB = 256
TM = 256
PV_ROWS = HEAD_DIM + 16
PG = 8

NEG_INF = float("-inf")
INT_MIN = np.int32(-2 ** 31)
KEY_NEG_INF = np.int32(-0x7F800000)


def _sortable(x):
    b = lax.bitcast_convert_type(x, I32)
    return jnp.where(b < 0, INT_MIN - b, b)


def _sigmoid(x):
    return 1.0 / (1.0 + jnp.exp(-x))


def _silu(x):
    return x * _sigmoid(x)


def _rms(x, g):
    ms = jnp.mean(x * x, axis=-1, keepdims=True)
    return (x * lax.rsqrt(ms + RMS_EPS)) * g


def _t5_bucket(dist):
    n = jnp.maximum(dist, 0)
    max_exact = N_BUCKETS // 2
    nf = jnp.maximum(n, 1).astype(F32)
    large = max_exact + (jnp.log(nf / max_exact) / np.log(MAX_DISTANCE / max_exact)
                         * (N_BUCKETS - max_exact)).astype(I32)
    large = jnp.minimum(large, N_BUCKETS - 1)
    return jnp.where(n < max_exact, n, large)


def _cparams(sem):
    return pltpu.CompilerParams(dimension_semantics=sem, vmem_limit_bytes=VMEM_LIMIT)


def _const_spec(shape):
    nd = len(shape)
    return pl.BlockSpec(shape, lambda *_: (0,) * nd)


def _inproj_prompt_body(x_ref, g_ref, wr_ref, wc_ref,
                        k_ref, v_ref, u_ref, zp_ref, ga_ref, gp_ref, ki_ref,
                        kb_ref, kib_ref, qT_ref, vT_ref, zaT_ref, qiT_ref, wT_ref):
    d_att = k_ref.shape[1]
    d_pool = u_ref.shape[1]
    d_model = ga_ref.shape[1]
    hb = _rms(x_ref[...], g_ref[...]).astype(BF16)

    def rowdot(a, n):
        return jnp.dot(hb, wr_ref[:, a:a + n], preferred_element_type=F32)

    def coldot(a, n):
        return lax.dot_general(wc_ref[a:a + n, :], hb, (((1,), (1,)), ((), ())),
                               preferred_element_type=F32)

    o = 0
    k = rowdot(o, d_att); o += d_att
    k_ref[...] = k
    kb_ref[...] = k.astype(BF16)
    v_ref[...] = rowdot(o, d_att); o += d_att
    u_ref[...] = rowdot(o, d_pool); o += d_pool
    zp_ref[...] = rowdot(o, d_pool); o += d_pool
    ga_ref[...] = rowdot(o, d_model); o += d_model
    gp_ref[...] = rowdot(o, d_model); o += d_model
    ki = rowdot(o, LANES)[:, :IDX_DIM]
    ki_ref[...] = ki
    kib_ref[...] = ki.astype(BF16)

    o = 0
    qT_ref[...] = coldot(o, d_att).astype(BF16); o += d_att
    vT = coldot(o, d_att).astype(BF16); o += d_att
    for j in range(vT_ref.shape[0]):
        vT_ref[j] = vT[:, j * LANES:(j + 1) * LANES]
    zaT_ref[...] = coldot(o, d_att); o += d_att
    qiT_ref[...] = coldot(o, N_IDX_HEADS * IDX_DIM).astype(BF16); o += N_IDX_HEADS * IDX_DIM
    wT_ref[...] = coldot(o, 16)[0:N_IDX_HEADS] * (N_IDX_HEADS ** -0.5)


def _inproj_prompt(x, g, wr, wc, d_att, d_pool):
    t, d = x.shape
    nb = t // TM
    row = lambda n: pl.BlockSpec((TM, n), lambda i: (i, 0))
    col = lambda n: pl.BlockSpec((n, TM), lambda i: (0, i))
    out_shape = (
        jax.ShapeDtypeStruct((t, d_att), F32),
        jax.ShapeDtypeStruct((t, d_att), F32),
        jax.ShapeDtypeStruct((t, d_pool), F32),
        jax.ShapeDtypeStruct((t, d_pool), F32),
        jax.ShapeDtypeStruct((t, d), F32),
        jax.ShapeDtypeStruct((t, d), F32),
        jax.ShapeDtypeStruct((t, IDX_DIM), F32),
        jax.ShapeDtypeStruct((t, d_att), BF16),
        jax.ShapeDtypeStruct((t, IDX_DIM), BF16),
        jax.ShapeDtypeStruct((d_att, t), BF16),
        jax.ShapeDtypeStruct((t // LANES, d_att, LANES), BF16),
        jax.ShapeDtypeStruct((d_att, t), F32),
        jax.ShapeDtypeStruct((N_IDX_HEADS * IDX_DIM, t), BF16),
        jax.ShapeDtypeStruct((N_IDX_HEADS, t), F32),
    )
    out_specs = (
        row(d_att), row(d_att), row(d_pool), row(d_pool), row(d), row(d), row(IDX_DIM),
        row(d_att), row(IDX_DIM), col(d_att),
        pl.BlockSpec((TM // LANES, d_att, LANES), lambda i: (i, 0, 0)),
        col(d_att), col(N_IDX_HEADS * IDX_DIM), col(N_IDX_HEADS),
    )
    return pl.pallas_call(
        _inproj_prompt_body,
        grid=(nb,),
        in_specs=[row(d), _const_spec((1, d)), _const_spec(wr.shape), _const_spec(wc.shape)],
        out_specs=out_specs,
        out_shape=out_shape,
        compiler_params=_cparams(("parallel",)),
        name="inproj_prompt",
    )(x, g, wr, wc)


def _attn_prompt_body(qT_ref, qiT_ref, wT_ref, kb_ref, kib_ref, vT_ref, bias_ref, o_ref,
                      key_ref, mask_ref, qz_ref, m_ref, acc_ref, *, topk, pos_bits):
    i = pl.program_id(1)
    nsb = i + 1
    nch = (TQ // CH) * (i + 1)

    row_k = lax.broadcasted_iota(I32, (CH, TQ), 0)
    lane_q = lax.broadcasted_iota(I32, (CH, TQ), 1) + i * TQ

    def idx_body(c, carry):
        r0 = pl.multiple_of(c * CH, CH)
        kc = kib_ref[pl.ds(r0, CH), :]
        sc = jnp.zeros((CH, TQ), F32)
        for h in range(N_IDX_HEADS):
            lg = jnp.dot(kc, qiT_ref[h * IDX_DIM:(h + 1) * IDX_DIM, :],
                         preferred_element_type=F32)
            sc = sc + wT_ref[h:h + 1, :] * jnp.maximum(lg, 0.0)
        sc = jnp.where(r0 + row_k <= lane_q, sc, NEG_INF)
        key_ref[pl.ds(r0, CH), :] = _sortable(sc)
        return carry

    lax.fori_loop(0, nch, idx_body, 0)

    def count(pred):
        def body(c, acc):
            r0 = pl.multiple_of(c * SB, SB)
            hit = jnp.where(pred(key_ref[pl.ds(r0, SB), :], r0), 1, 0)
            return acc + hit.reshape(SB // SUBLANES, SUBLANES, TQ).sum(axis=0)
        acc = lax.fori_loop(0, nsb, body, jnp.zeros((SUBLANES, TQ), I32))
        return jnp.sum(acc, axis=0, keepdims=True)

    def count_ge(cand):
        return count(lambda kk, r0: kk >= cand)

    t0 = jnp.where(count_ge(jnp.zeros((1, TQ), I32)) >= topk,
                   jnp.zeros((1, TQ), I32), jnp.full((1, TQ), INT_MIN, I32))

    def bit_body(s, t):
        cand = t | jnp.left_shift(jnp.int32(1), 30 - s)
        return jnp.where(count_ge(cand) >= topk, cand, t)

    thr = lax.fori_loop(0, 31, bit_body, t0)

    cnt_gt = count(lambda kk, r0: kk > thr)
    cnt_eq = count(lambda kk, r0: kk == thr)
    need = topk - cnt_gt
    tie_ok = thr > KEY_NEG_INF
    excess = jnp.logical_and(cnt_eq > need, tie_ok)
    any_excess = jnp.max(jnp.where(excess, 1, 0)) > 0

    @pl.when(jnp.logical_not(any_excess))
    def _():
        thr_eff = jnp.maximum(thr, KEY_NEG_INF + 1)

        def body(c, carry):
            r0 = pl.multiple_of(c * SB, SB)
            mask_ref[pl.ds(r0, SB), :] = jnp.where(key_ref[pl.ds(r0, SB), :] >= thr_eff,
                                                   0.0, NEG_INF)
            return carry
        lax.fori_loop(0, nsb, body, 0)

    @pl.when(any_excess)
    def _():
        row_sb = lax.broadcasted_iota(I32, (SB, TQ), 0)

        def pos_body(s, p):
            cand = p + jnp.left_shift(jnp.int32(1), pos_bits - 1 - s)
            cnt = count(lambda kk, r0: jnp.logical_and(kk == thr, r0 + row_sb < cand))
            return jnp.where(cnt <= need - 1, cand, p)
        last = lax.fori_loop(0, pos_bits, pos_body, jnp.zeros((1, TQ), I32))

        def body(c, carry):
            r0 = pl.multiple_of(c * SB, SB)
            kk = key_ref[pl.ds(r0, SB), :]
            tie = jnp.logical_and(jnp.logical_and(kk == thr, r0 + row_sb <= last), tie_ok)
            sel = jnp.logical_and(jnp.logical_or(kk > thr, tie), kk > KEY_NEG_INF)
            mask_ref[pl.ds(r0, SB), :] = jnp.where(sel, 0.0, NEG_INF)
            return carry
        lax.fori_loop(0, nsb, body, 0)

    zero_half = jnp.zeros((HEAD_DIM, TQ), BF16)
    for h in range(N_HEADS):
        own = qT_ref[h * HEAD_DIM:(h + 1) * HEAD_DIM, :]
        qz_ref[h] = jnp.concatenate([own, zero_half] if h % 2 == 0 else [zero_half, own], axis=0)
    m_ref[...] = jnp.full(m_ref.shape, NEG_INF, F32)
    acc_ref[...] = jnp.zeros(acc_ref.shape, F32)
    ones_rows = jnp.ones((PV_ROWS - HEAD_DIM, CH), BF16)

    def att_body(c, carry):
        r0 = pl.multiple_of(c * CH, CH)
        jj = jnp.minimum(nch - 1 - c, bias_ref.shape[0] - 1)
        msk = mask_ref[pl.ds(r0, CH), :]
        for h in range(N_HEADS):
            pair = h // 2
            kp = kb_ref[pl.ds(r0, CH), pair * 2 * HEAD_DIM:(pair + 1) * 2 * HEAD_DIM]
            s = jnp.dot(kp, qz_ref[h], preferred_element_type=F32) + bias_ref[jj, h] + msk
            m_old = m_ref[h:h + 1, :]
            m_new = jnp.maximum(m_old, jnp.max(s, axis=0, keepdims=True))
            m_safe = jnp.where(m_new == NEG_INF, 0.0, m_new)
            alpha = jnp.exp(m_old - m_safe)
            p = jnp.exp(s - m_safe).astype(BF16)
            vt = jnp.concatenate([vT_ref[c, h * HEAD_DIM:(h + 1) * HEAD_DIM, :], ones_rows], axis=0)
            pv = jnp.dot(vt, p, preferred_element_type=F32)
            rows = slice(h * PV_ROWS, (h + 1) * PV_ROWS)
            acc_ref[rows, :] = acc_ref[rows, :] * alpha + pv
            m_ref[h:h + 1, :] = m_new
        return carry

    lax.fori_loop(0, nch, att_body, 0)

    for h in range(N_HEADS):
        num = acc_ref[h * PV_ROWS:h * PV_ROWS + HEAD_DIM, :]
        den = acc_ref[h * PV_ROWS + HEAD_DIM:h * PV_ROWS + HEAD_DIM + 1, :]
        o_ref[h * HEAD_DIM:(h + 1) * HEAD_DIM, :] = num / den


def _attn_prompt(qT, qiT, wT, kb, kib, vT, bias_tab, batch, seq):
    d_att = qT.shape[0]
    nq = seq // TQ
    topk = min(TOPK_MAX, seq // 4)
    pos_bits = max(1, (seq - 1).bit_length())
    body = functools.partial(_attn_prompt_body, topk=topk, pos_bits=pos_bits)
    qcol = lambda n: pl.BlockSpec((n, TQ), lambda b, i: (0, b * nq + i))
    return pl.pallas_call(
        body,
        grid=(batch, nq),
        in_specs=[
            qcol(d_att), qcol(N_IDX_HEADS * IDX_DIM), qcol(N_IDX_HEADS),
            pl.BlockSpec((seq, d_att), lambda b, i: (b, 0)),
            pl.BlockSpec((seq, IDX_DIM), lambda b, i: (b, 0)),
            pl.BlockSpec((seq // LANES, d_att, LANES), lambda b, i: (b, 0, 0)),
            _const_spec(bias_tab.shape),
        ],
        out_specs=qcol(d_att),
        out_shape=jax.ShapeDtypeStruct((d_att, batch * seq), F32),
        scratch_shapes=[
            pltpu.VMEM((seq, TQ), I32),
            pltpu.VMEM((seq, TQ), F32),
            pltpu.VMEM((N_HEADS, 2 * HEAD_DIM, TQ), BF16),
            pltpu.VMEM((N_HEADS, TQ), F32),
            pltpu.VMEM((N_HEADS * PV_ROWS, TQ), F32),
        ],
        compiler_params=_cparams(("parallel", "arbitrary")),
        name="attn_prompt",
    )(qT, qiT, wT, kb, kib, vT, bias_tab)


def _merge_tail(x, a_z, p_out, zp, ga, gp, bg_ref, wua_ref, wup_ref, wo_ref):
    a = jnp.dot(a_z.astype(BF16), wua_ref[...], preferred_element_type=F32)
    p = jnp.dot((p_out * _silu(zp)).astype(BF16), wup_ref[...], preferred_element_type=F32)
    mix = _sigmoid(ga + bg_ref[0:1, :]) * a + _sigmoid(gp + bg_ref[1:2, :]) * p
    return x + jnp.dot(mix.astype(BF16), wo_ref[...], preferred_element_type=F32)


def _merge_prompt_body(x_ref, aT_ref, zaT_ref, u_ref, uh_ref, zp_ref, ga_ref, gp_ref,
                       wg_ref, ps_ref, bg_ref, wua_ref, wup_ref, wo_ref, fg_ref,
                       xo_ref, y_ref, ue_ref, *, blocks_per_seq):
    i = pl.program_id(0)
    blk = i % blocks_per_seq
    hist = uh_ref.shape[0]
    ue_ref[0:hist, :] = jnp.where(blk == 0, 0.0, uh_ref[...])
    ue_ref[hist:hist + TM, :] = u_ref[...]
    t_in_seq = blk * TM + lax.broadcasted_iota(I32, (TM, POOL_GROUP), 0)
    parts = []
    for g, w in enumerate(POOL_WINDOWS):
        cols = slice(g * POOL_GROUP, (g + 1) * POOL_GROUP)
        ssum = ue_ref[hist:hist + TM, cols]
        for j in range(1, w):
            ssum = ssum + ue_ref[hist - j:hist - j + TM, cols]
        cnt = jnp.minimum(t_in_seq + 1, w).astype(F32)
        diff = ssum / cnt - u_ref[:, cols]
        parts.append(jnp.dot(diff.astype(BF16), wg_ref[g], preferred_element_type=F32))
    p_out = jnp.concatenate(parts, axis=1) * ps_ref[...]
    a_z = (aT_ref[...] * _silu(zaT_ref[...])).T
    xn = _merge_tail(x_ref[...], a_z, p_out, zp_ref[...], ga_ref[...], gp_ref[...],
                     bg_ref, wua_ref, wup_ref, wo_ref)
    xo_ref[...] = xn
    y_ref[...] = _rms(xn, fg_ref[...])


def _merge_prompt(x, aT, zaT, u, zp, ga, gp, wg, ps, bg, wua, wup, wo, fg, seq):
    t, d = x.shape
    d_att = aT.shape[0]
    d_pool = u.shape[1]
    hist = 16
    per = TM // hist
    row = lambda n: pl.BlockSpec((TM, n), lambda i: (i, 0))
    col = lambda n: pl.BlockSpec((n, TM), lambda i: (0, i))
    body = functools.partial(_merge_prompt_body, blocks_per_seq=seq // TM)
    return pl.pallas_call(
        body,
        grid=(t // TM,),
        in_specs=[
            row(d), col(d_att), col(d_att), row(d_pool),
            pl.BlockSpec((hist, d_pool), lambda i: (jnp.maximum(i * per - 1, 0), 0)),
            row(d_pool), row(d), row(d),
            _const_spec(wg.shape), _const_spec(ps.shape), _const_spec(bg.shape),
            _const_spec(wua.shape), _const_spec(wup.shape), _const_spec(wo.shape),
            _const_spec(fg.shape),
        ],
        out_specs=(row(d), row(d)),
        out_shape=(jax.ShapeDtypeStruct((t, d), F32), jax.ShapeDtypeStruct((t, d), F32)),
        scratch_shapes=[pltpu.VMEM((hist + TM, d_pool), F32)],
        compiler_params=_cparams(("parallel",)),
        name="merge_prompt",
    )(x, aT, zaT, u, u, zp, ga, gp, wg, ps, bg, wua, wup, wo, fg)


def _merge_sample_body(x_ref, a_ref, za_ref, ue_ref, zp_ref, ga_ref, gp_ref,
                       wg_ref, ps_ref, bg_ref, wua_ref, wup_ref, wo_ref, fg_ref,
                       xo_ref, y_ref):
    db, ext, _ = ue_ref.shape
    tn = ext - 16
    parts = []
    for g, w in enumerate(POOL_WINDOWS):
        cols = slice(g * POOL_GROUP, (g + 1) * POOL_GROUP)
        cur = ue_ref[:, 16:16 + tn, cols]
        ssum = cur
        for j in range(1, w):
            ssum = ssum + ue_ref[:, 16 - j:16 - j + tn, cols]
        diff = (ssum / float(w) - cur).reshape(db * tn, POOL_GROUP)
        parts.append(jnp.dot(diff.astype(BF16), wg_ref[g], preferred_element_type=F32))
    p_out = jnp.concatenate(parts, axis=1) * ps_ref[...]
    a_z = a_ref[...] * _silu(za_ref[...])
    xn = _merge_tail(x_ref[...], a_z, p_out, zp_ref[...], ga_ref[...], gp_ref[...],
                     bg_ref, wua_ref, wup_ref, wo_ref)
    xo_ref[...] = xn
    y_ref[...] = _rms(xn, fg_ref[...])


def _merge_sample(x, a, za, ue, zp, ga, gp, wg, ps, bg, wua, wup, wo, fg):
    t, d = x.shape
    args = (x, a, za, ue, zp, ga, gp, wg, ps, bg, wua, wup, wo, fg)
    return pl.pallas_call(
        _merge_sample_body,
        grid=(1,),
        in_specs=[_const_spec(v.shape) for v in args],
        out_specs=(_const_spec((t, d)), _const_spec((t, d))),
        out_shape=(jax.ShapeDtypeStruct((t, d), F32), jax.ShapeDtypeStruct((t, d), F32)),
        compiler_params=_cparams(("arbitrary",)),
        name="merge_sample",
    )(*args)


def _inproj_sample_body(x_ref, g_ref, w_ref, o_ref):
    hb = _rms(x_ref[...], g_ref[...]).astype(BF16)
    o_ref[...] = jnp.dot(hb, w_ref[...], preferred_element_type=F32)


def _inproj_sample(x, g, w, nblk):
    t, d = x.shape
    n = w.shape[1]
    bn = n // nblk
    return pl.pallas_call(
        _inproj_sample_body,
        grid=(nblk,),
        in_specs=[_const_spec((t, d)), _const_spec((1, d)), pl.BlockSpec((d, bn), lambda j: (0, j))],
        out_specs=pl.BlockSpec((t, bn), lambda j: (0, j)),
        out_shape=jax.ShapeDtypeStruct((t, n), F32),
        compiler_params=_cparams(("parallel",)),
        name="inproj_sample",
    )(x, g, w)


def _idx_page_scores(qi, w, kpage_bf):
    lg = lax.dot_general(qi, kpage_bf, (((1,), (1,)), ((), ())), preferred_element_type=F32)
    lg = jnp.maximum(lg, 0.0) * w
    tn = lg.shape[0] // N_IDX_HEADS
    return lg.reshape(N_IDX_HEADS, tn, lg.shape[1]).sum(axis=0)


def _sidx_body(pt_ref, qi_ref, w_ref, *rest):
    c_refs, o_ref = rest[:PG], rest[PG]
    qi = qi_ref[...]
    w = w_ref[...]
    for g in range(PG):
        o_ref[:, g * PAGE:(g + 1) * PAGE] = _idx_page_scores(qi, w, c_refs[g][...].astype(BF16))


def _sample_idx_scores(page_tab, qi, w, cache_idx, layer, n_pages):
    db, rows, _ = qi.shape
    tn = rows // N_IDX_HEADS
    npg = n_pages // PG

    def page_spec(g):
        return pl.BlockSpec((None, None, PAGE, IDX_DIM),
                            lambda b, j, pt: (layer, pt[b * n_pages + j * PG + g], 0, 0))

    grid_spec = pltpu.PrefetchScalarGridSpec(
        num_scalar_prefetch=1,
        grid=(db, npg),
        in_specs=[pl.BlockSpec((None, rows, IDX_DIM), lambda b, j, pt: (b, 0, 0)),
                  pl.BlockSpec((None, rows, 1), lambda b, j, pt: (b, 0, 0))]
                 + [page_spec(g) for g in range(PG)],
        out_specs=pl.BlockSpec((None, tn, PG * PAGE), lambda b, j, pt: (b, 0, j)),
    )
    return pl.pallas_call(
        _sidx_body,
        grid_spec=grid_spec,
        out_shape=jax.ShapeDtypeStruct((db, tn, n_pages * PAGE), F32),
        compiler_params=_cparams(("parallel", "arbitrary")),
        name="sample_idx",
    )(page_tab, qi, w, *([cache_idx] * PG))


def _sattn_body(pt_ref, sp_ref, qi_ref, w_ref, kin_ref, q_ref, kn_ref, vn_ref, bias_ref, *rest,
                topk, pos_bits, n_pages):
    k_refs, v_refs = rest[:PG], rest[PG:2 * PG]
    o_ref, key_ref, mask_ref, m_ref, l_ref, acc_ref = rest[2 * PG:]
    j = pl.program_id(1)
    nchk = n_pages + 1
    tn = o_ref.shape[0]

    @pl.when(j == 0)
    def _():
        for c in range(n_pages):
            key_ref[c] = _sortable(sp_ref[:, c * PAGE:(c + 1) * PAGE])
        sc_new = _idx_page_scores(qi_ref[...], w_ref[...], kin_ref[...])
        col = lax.broadcasted_iota(I32, (tn, PAGE), 1)
        row = lax.broadcasted_iota(I32, (tn, PAGE), 0)
        key_ref[n_pages] = _sortable(jnp.where(col <= row, sc_new, NEG_INF))

        def count(pred):
            acc = jnp.zeros((tn, PAGE), I32)
            for c in range(nchk):
                acc = acc + jnp.where(pred(key_ref[c], c), 1, 0)
            return jnp.broadcast_to(jnp.sum(acc, axis=1, keepdims=True), (tn, PAGE))

        zeros = jnp.zeros((tn, PAGE), I32)
        t0 = jnp.where(count(lambda kk, c: kk >= zeros) >= topk, zeros,
                       jnp.full((tn, PAGE), INT_MIN, I32))

        def bit_body(s, t):
            cand = t | jnp.left_shift(jnp.int32(1), 30 - s)
            return jnp.where(count(lambda kk, c: kk >= cand) >= topk, cand, t)
        thr = lax.fori_loop(0, 31, bit_body, t0)

        cnt_gt = count(lambda kk, c: kk > thr)
        cnt_eq = count(lambda kk, c: kk == thr)
        need = topk - cnt_gt
        tie_ok = thr > KEY_NEG_INF
        excess = jnp.logical_and(cnt_eq > need, tie_ok)
        any_excess = jnp.max(jnp.where(excess, 1, 0)) > 0

        @pl.when(jnp.logical_not(any_excess))
        def _():
            thr_eff = jnp.maximum(thr, KEY_NEG_INF + 1)
            for c in range(nchk):
                mask_ref[c] = jnp.where(key_ref[c] >= thr_eff, 0.0, NEG_INF)

        @pl.when(any_excess)
        def _():
            def pos_body(s, p):
                cand = p + jnp.left_shift(jnp.int32(1), pos_bits - 1 - s)
                cnt = count(lambda kk, c: jnp.logical_and(kk == thr, c * PAGE + col < cand))
                return jnp.where(cnt <= need - 1, cand, p)
            last = lax.fori_loop(0, pos_bits, pos_body, zeros)
            for c in range(nchk):
                kk = key_ref[c]
                tie = jnp.logical_and(jnp.logical_and(kk == thr, c * PAGE + col <= last), tie_ok)
                sel = jnp.logical_and(jnp.logical_or(kk > thr, tie), kk > KEY_NEG_INF)
                mask_ref[c] = jnp.where(sel, 0.0, NEG_INF)

        m_ref[...] = jnp.full(m_ref.shape, NEG_INF, F32)
        l_ref[...] = jnp.zeros(l_ref.shape, F32)
        acc_ref[...] = jnp.zeros(acc_ref.shape, F32)

    def attend(h, k_bf, v_bf, bias, msk):
        s = lax.dot_general(q_ref[h], k_bf, (((1,), (1,)), ((), ())),
                            preferred_element_type=F32) + bias + msk
        m_old = m_ref[h]
        m_new = jnp.maximum(m_old, jnp.max(s, axis=1, keepdims=True))
        m_safe = jnp.where(m_new == NEG_INF, 0.0, m_new)
        alpha = jnp.exp(m_old - m_safe)
        p = jnp.exp(s - m_safe)
        l_ref[h] = l_ref[h] * alpha + jnp.sum(p, axis=1, keepdims=True)
        acc_ref[h] = acc_ref[h] * alpha + jnp.dot(p.astype(BF16), v_bf, preferred_element_type=F32)
        m_ref[h] = m_new

    for g in range(PG):
        pg = j * PG + g
        msk = mask_ref[pg]
        jj = jnp.where(pg == n_pages - 1, 1, 0)
        for h in range(N_HEADS):
            k_bf = k_refs[g][pl.ds(h, PAGE, stride=N_HEADS), :].astype(BF16)
            v_bf = v_refs[g][pl.ds(h, PAGE, stride=N_HEADS), :].astype(BF16)
            attend(h, k_bf, v_bf, bias_ref[jj, h], msk)

    @pl.when(j == pl.num_programs(1) - 1)
    def _():
        msk = mask_ref[n_pages]
        for h in range(N_HEADS):
            attend(h, kn_ref[h], vn_ref[h], bias_ref[2, h], msk)
            o_ref[:, h * HEAD_DIM:(h + 1) * HEAD_DIM] = acc_ref[h] / l_ref[h]


def _sample_attn(page_tab, sp, qi, w, kin, q, kn, vn, bias_s, cache_k, cache_v, layer, n_pages):
    db, tn, past = sp.shape
    rows = qi.shape[1]
    npg = n_pages // PG
    topk = min(TOPK_MAX, (past + tn) // 4)
    pos_bits = (past + PAGE - 1).bit_length()
    body = functools.partial(_sattn_body, topk=topk, pos_bits=pos_bits, n_pages=n_pages)

    def page_spec(g):
        return pl.BlockSpec((None, None, PAGE * N_HEADS, HEAD_DIM),
                            lambda b, j, pt: (layer, pt[b * n_pages + j * PG + g], 0, 0))

    per_b = lambda shape: pl.BlockSpec((None,) + shape, lambda b, j, pt: (b,) + (0,) * len(shape))
    grid_spec = pltpu.PrefetchScalarGridSpec(
        num_scalar_prefetch=1,
        grid=(db, npg),
        in_specs=[per_b((tn, past)), per_b((rows, IDX_DIM)), per_b((rows, 1)),
                  per_b((PAGE, IDX_DIM)), per_b((N_HEADS, tn, HEAD_DIM)),
                  per_b((N_HEADS, PAGE, HEAD_DIM)), per_b((N_HEADS, PAGE, HEAD_DIM)),
                  pl.BlockSpec(bias_s.shape, lambda b, j, pt: (0,) * 4)]
                 + [page_spec(g) for g in range(PG)] * 2,
        out_specs=per_b((tn, N_HEADS * HEAD_DIM)),
        scratch_shapes=[
            pltpu.VMEM((n_pages + 1, tn, PAGE), I32),
            pltpu.VMEM((n_pages + 1, tn, PAGE), F32),
            pltpu.VMEM((N_HEADS, tn, 1), F32),
            pltpu.VMEM((N_HEADS, tn, 1), F32),
            pltpu.VMEM((N_HEADS, tn, HEAD_DIM), F32),
        ],
    )
    return pl.pallas_call(
        body,
        grid_spec=grid_spec,
        out_shape=jax.ShapeDtypeStruct((db, tn, N_HEADS * HEAD_DIM), F32),
        compiler_params=_cparams(("parallel", "arbitrary")),
        name="sample_attn",
    )(page_tab, sp, qi, w, kin, q, kn, vn, bias_s, *([cache_k] * PG), *([cache_v] * PG))


def kernel(x_prompt, x_sample, cache_k, cache_v, cache_idx, state_pool, page_table, rel_bias,
           norm_g, w_in, b_gate, w_pool_group, pool_scale, w_up_attn, w_up_pool, w_out,
           final_norm_g):
    batch, seq, d = x_prompt.shape
    db, tn, _ = x_sample.shape
    depth = w_in.shape[0]
    n_pages = page_table.shape[1]
    past = n_pages * PAGE
    d_att = N_HEADS * HEAD_DIM
    d_pool = pool_scale.shape[1]
    d_qi = N_IDX_HEADS * IDX_DIM
    sizes = (d_att, d_att, d_att, d_att, d_qi, IDX_DIM, N_IDX_HEADS, d_pool, d_pool, d, d)
    offs = np.concatenate([[0], np.cumsum(sizes)])
    (o_q, o_k, o_v, o_za, o_qi, o_ki, o_w, o_u, o_zp, o_ga, o_gp) = [int(v) for v in offs[:-1]]
    d_in = int(offs[-1])
    assert w_in.shape[2] == d_in and seq % TQ == 0 and (batch * seq) % TM == 0
    assert n_pages % PG == 0 and tn == SUBLANES

    def seg(w, o, n):
        return w[:, :, o:o + n]

    w_row = jnp.concatenate([
        seg(w_in, o_k, d_att), seg(w_in, o_v, d_att), seg(w_in, o_u, d_pool), seg(w_in, o_zp, d_pool),
        seg(w_in, o_ga, d), seg(w_in, o_gp, d), seg(w_in, o_ki, IDX_DIM),
        jnp.zeros((depth, d, LANES - IDX_DIM), F32)], axis=2).astype(BF16)
    w_col = jnp.swapaxes(jnp.concatenate([
        seg(w_in, o_q, d_att) * HEAD_DIM ** -0.5, seg(w_in, o_v, d_att), seg(w_in, o_za, d_att),
        seg(w_in, o_qi, d_qi) * IDX_DIM ** -0.5, seg(w_in, o_w, N_IDX_HEADS),
        jnp.zeros((depth, d, 16 - N_IDX_HEADS), F32)], axis=2), 1, 2).astype(BF16)
    n_pad = -d_in % (5 * LANES)
    w_all = jnp.pad(w_in, ((0, 0), (0, 0), (0, n_pad))).astype(BF16)
    wg = w_pool_group.astype(BF16)
    wua = w_up_attn.astype(BF16)
    wup = w_up_pool.astype(BF16)
    wo = w_out.astype(BF16)
    fg = final_norm_g.reshape(1, d)

    r = jnp.arange(CH, dtype=I32)[:, None]
    qq = jnp.arange(TQ, dtype=I32)[None, :]
    near = 1 + TQ // CH
    dist_p = jnp.stack([qq - r + CH * (jx + 1) - TQ for jx in range(near + 1)])
    bias_p = jnp.transpose(rel_bias[_t5_bucket(dist_p)], (0, 3, 1, 2))
    tt = jnp.arange(tn, dtype=I32)[:, None]
    ss = jnp.arange(PAGE, dtype=I32)[None, :]
    dist_s = jnp.stack([jnp.full((tn, PAGE), 2 * MAX_DISTANCE, I32), tt + PAGE - ss, tt - ss])
    bias_s = jnp.transpose(rel_bias[_t5_bucket(dist_s)], (0, 3, 1, 2))

    ck = cache_k.reshape(depth, cache_k.shape[1], PAGE * N_HEADS, HEAD_DIM)
    cv = cache_v.reshape(depth, cache_v.shape[1], PAGE * N_HEADS, HEAD_DIM)
    page_flat = page_table.reshape(-1).astype(I32)

    xp = x_prompt.reshape(batch * seq, d)
    xs = x_sample.reshape(db * tn, d)
    outs = [[] for _ in range(8)]
    yp = ys = None
    for l in range(depth):
        g_l = norm_g[l].reshape(1, d)
        (k, v, u, zp, ga, gp, ki, kb, kib, qT, vT, zaT, qiT, wT) = _inproj_prompt(
            xp, g_l, w_row[l], w_col[l], d_att, d_pool)
        aT = _attn_prompt(qT, qiT, wT, kb, kib, vT, bias_p, batch, seq)
        xp, yp = _merge_prompt(xp, aT, zaT, u, zp, ga, gp, wg[l], pool_scale[l].reshape(1, d_pool),
                               b_gate[l], wua[l], wup[l], wo[l], fg, seq)
        outs[0].append(k.reshape(batch, seq, N_HEADS, HEAD_DIM))
        outs[1].append(v.reshape(batch, seq, N_HEADS, HEAD_DIM))
        outs[2].append(ki.reshape(batch, seq, IDX_DIM))
        outs[3].append(u.reshape(batch, seq, d_pool)[:, seq - POOL_HIST:])
        ps = _inproj_sample(xs, g_l, w_all[l], 5)
        cut = lambda o, n: ps[:, o:o + n]
        k_s, v_s, ki_s, u_s = cut(o_k, d_att), cut(o_v, d_att), cut(o_ki, IDX_DIM), cut(o_u, d_pool)
        heads = lambda a, hd: jnp.transpose(a.reshape(db, tn, -1, hd), (0, 2, 1, 3))
        q_s = (heads(cut(o_q, d_att), HEAD_DIM) * HEAD_DIM ** -0.5).astype(BF16)
        qi_s = (heads(cut(o_qi, d_qi), IDX_DIM) * IDX_DIM ** -0.5).astype(BF16).reshape(db, -1, IDX_DIM)
        w_s = jnp.transpose(cut(o_w, N_IDX_HEADS).reshape(db, tn, N_IDX_HEADS) * N_IDX_HEADS ** -0.5,
                            (0, 2, 1)).reshape(db, -1, 1)
        pad_keys = lambda a: jnp.pad(a.astype(BF16), [(0, 0)] * (a.ndim - 2) + [(0, PAGE - tn), (0, 0)])
        kin = pad_keys(ki_s.reshape(db, tn, IDX_DIM))
        kn = pad_keys(heads(k_s, HEAD_DIM))
        vn = pad_keys(heads(v_s, HEAD_DIM))
        sp = _sample_idx_scores(page_flat, qi_s, w_s, cache_idx, l, n_pages)
        a_s = _sample_attn(page_flat, sp, qi_s, w_s, kin, q_s, kn, vn, bias_s, ck, cv, l, n_pages)
        u_ext = jnp.concatenate([state_pool[l], u_s.reshape(db, tn, d_pool)], axis=1)
        ue = jnp.pad(u_ext, ((0, 0), (16 - POOL_HIST, 0), (0, 0)))
        xs, ys = _merge_sample(xs, a_s.reshape(db * tn, d_att), cut(o_za, d_att), ue,
                               cut(o_zp, d_pool), cut(o_ga, d), cut(o_gp, d), wg[l],
                               pool_scale[l].reshape(1, d_pool), b_gate[l], wua[l], wup[l], wo[l], fg)
        outs[4].append(k_s.reshape(db, tn, N_HEADS, HEAD_DIM))
        outs[5].append(v_s.reshape(db, tn, N_HEADS, HEAD_DIM))
        outs[6].append(ki_s.reshape(db, tn, IDX_DIM))
        outs[7].append(u_ext[:, tn:])
    return (yp.reshape(batch, seq, d), ys.reshape(db, tn, d)) + tuple(jnp.stack(o) for o in outs)
```

```python
import functools

import numpy as np
import jax
import jax.numpy as jnp
from jax import lax
from jax.experimental import pallas as pl
from jax.experimental.pallas import tpu as pltpu

F32 = jnp.float32
BF16 = jnp.bfloat16
I32 = jnp.int32

N_HEADS = 8
HEAD_DIM = 64
D_ATT = N_HEADS * HEAD_DIM
N_IDX_HEADS = 8
IDX_DIM = 64
D_QI = N_IDX_HEADS * IDX_DIM
TOPK_MAX = 256
PAGE = 128
POOL_WINDOWS = (2, 4, 8, 16)
POOL_GROUP = 128
POOL_HIST = 15
N_BUCKETS = 32
MAX_DISTANCE = 128
RMS_EPS = 1e-6
LOG2E = 1.4426950408889634

LANES = 128
SUBLANES = 8
VMEM_LIMIT = 56 * 1024 * 1024

TQ = 256
CH = 128
SB = 256
TM = 256
NEAR = 2 + TQ // CH
PV_ROWS = HEAD_DIM + 16
PG = 8
PGI = 16
HIST_PAD = 16

NEG_INF = float("-inf")
INT_MIN = np.int32(-2 ** 31)
KEY_NEG_INF = np.int32(-0x7F800000)


def _sortable(x):
    b = lax.bitcast_convert_type(x, I32)
    return jnp.where(b < 0, INT_MIN - b, b)


def _sigmoid(x):
    return 1.0 / (1.0 + jnp.exp(-x))


def _silu(x):
    return x * _sigmoid(x)


def _rms(x, g):
    ms = jnp.mean(x * x, axis=-1, keepdims=True)
    return (x * lax.rsqrt(ms + RMS_EPS)) * g


def _t5_bucket(dist):
    n = jnp.maximum(dist, 0)
    max_exact = N_BUCKETS // 2
    nf = jnp.maximum(n, 1).astype(F32)
    large = max_exact + (jnp.log(nf / max_exact) / np.log(MAX_DISTANCE / max_exact)
                         * (N_BUCKETS - max_exact)).astype(I32)
    large = jnp.minimum(large, N_BUCKETS - 1)
    return jnp.where(n < max_exact, n, large)


def _dot_nt(a, b):
    return lax.dot_general(a, b, (((1,), (1,)), ((), ())), preferred_element_type=F32)


def _cparams(sem):
    return pltpu.CompilerParams(dimension_semantics=sem, vmem_limit_bytes=VMEM_LIMIT)


def _const_spec(shape):
    nd = len(shape)
    return pl.BlockSpec(shape, lambda *_: (0,) * nd)


def _radix_select(count, topk, shape):
    zeros = jnp.zeros(shape, I32)
    t0 = jnp.where(count(lambda kk, pos: kk >= zeros) >= topk, zeros, jnp.full(shape, INT_MIN, I32))

    def bit_body(s, t):
        cand = t | jnp.left_shift(jnp.int32(1), 30 - s)
        return jnp.where(count(lambda kk, pos: kk >= cand) >= topk, cand, t)

    return lax.fori_loop(0, 31, bit_body, t0)


def _tie_rule(count, thr, topk):
    need = topk - count(lambda kk, pos: kk > thr)
    tie_ok = thr > KEY_NEG_INF
    excess = jnp.logical_and(count(lambda kk, pos: kk == thr) > need, tie_ok)
    return need, tie_ok, jnp.max(jnp.where(excess, 1, 0)) > 0


def _last_tie_position(count, thr, need, pos_bits, shape):
    def pos_body(s, p):
        cand = p + jnp.left_shift(jnp.int32(1), pos_bits - 1 - s)
        cnt = count(lambda kk, pos: jnp.logical_and(kk == thr, pos < cand))
        return jnp.where(cnt <= need - 1, cand, p)
    return lax.fori_loop(0, pos_bits, pos_body, jnp.zeros(shape, I32))


def _tie_mask(kk, pos, thr, last, tie_ok):
    tie = jnp.logical_and(jnp.logical_and(kk == thr, pos <= last), tie_ok)
    sel = jnp.logical_and(jnp.logical_or(kk > thr, tie), kk > KEY_NEG_INF)
    return jnp.where(sel, 0.0, NEG_INF)


def _inproj_prompt_body(x_ref, g_ref, wr_ref, wc_ref, kT_in, vT_in, kiT_in,
                        kT_ref, vT_ref, kiT_ref, u_ref, zp_ref, ga_ref, gp_ref,
                        kb_ref, kib_ref, qT_ref, vTc_ref, zaT_ref, qiT_ref, wT_ref):
    del kT_in, vT_in, kiT_in
    d_pool = u_ref.shape[1]
    d_model = ga_ref.shape[1]
    hb = _rms(x_ref[...], g_ref[...]).astype(BF16)

    def rowdot(a, n):
        return jnp.dot(hb, wr_ref[:, a:a + n], preferred_element_type=F32)

    def coldot(a, n):
        return _dot_nt(wc_ref[a:a + n, :], hb)

    o = 0
    kb_ref[...] = rowdot(o, D_ATT).astype(BF16); o += D_ATT
    u_ref[...] = rowdot(o, d_pool); o += d_pool
    zp_ref[...] = rowdot(o, d_pool); o += d_pool
    ga_ref[...] = rowdot(o, d_model); o += d_model
    gp_ref[...] = rowdot(o, d_model); o += d_model
    kib_ref[...] = rowdot(o, LANES)[:, :IDX_DIM].astype(BF16)

    o = 0
    qT_ref[...] = coldot(o, D_ATT).astype(BF16); o += D_ATT
    kT_ref[...] = coldot(o, D_ATT); o += D_ATT
    vT = coldot(o, D_ATT); o += D_ATT
    vT_ref[...] = vT
    vTb = vT.astype(BF16)
    for j in range(vTc_ref.shape[0]):
        vTc_ref[j] = vTb[:, j * LANES:(j + 1) * LANES]
    zaT_ref[...] = coldot(o, D_ATT); o += D_ATT
    qiT_ref[...] = coldot(o, D_QI).astype(BF16); o += D_QI
    kiT_ref[...] = coldot(o, IDX_DIM); o += IDX_DIM
    wT_ref[...] = coldot(o, 16)[0:N_IDX_HEADS] * (N_IDX_HEADS ** -0.5)


def _inproj_prompt(x, g, wr, wc, kT_all, vT_all, kiT_all, layer, batch, seq, d_pool):
    t, d = x.shape
    per = seq // TM
    row = lambda n: pl.BlockSpec((TM, n), lambda i: (i, 0))
    col = lambda n: pl.BlockSpec((n, TM), lambda i: (0, i))
    stacked = lambda n: pl.BlockSpec((None, None, n, TM), lambda i: (layer, i // per, 0, i % per))
    hbm = pl.BlockSpec(memory_space=pl.ANY)
    sds = jax.ShapeDtypeStruct
    out_shape = (
        sds(kT_all.shape, F32), sds(vT_all.shape, F32), sds(kiT_all.shape, F32),
        sds((t, d_pool), F32),
        sds((t, d_pool), F32),
        sds((t, d), F32),
        sds((t, d), F32),
        sds((t, D_ATT), BF16),
        sds((t, IDX_DIM), BF16),
        sds((D_ATT, t), BF16),
        sds((t // LANES, D_ATT, LANES), BF16),
        sds((D_ATT, t), F32),
        sds((D_QI, t), BF16),
        sds((N_IDX_HEADS, t), F32),
    )
    out_specs = (
        stacked(D_ATT), stacked(D_ATT), stacked(IDX_DIM),
        row(d_pool), row(d_pool), row(d), row(d), row(D_ATT), row(IDX_DIM), col(D_ATT),
        pl.BlockSpec((TM // LANES, D_ATT, LANES), lambda i: (i, 0, 0)),
        col(D_ATT), col(D_QI), col(N_IDX_HEADS),
    )
    return pl.pallas_call(
        _inproj_prompt_body,
        grid=(t // TM,),
        in_specs=[row(d), _const_spec((1, d)), _const_spec(wr.shape), _const_spec(wc.shape),
                  hbm, hbm, hbm],
        out_specs=out_specs,
        out_shape=out_shape,
        input_output_aliases={4: 0, 5: 1, 6: 2},
        compiler_params=_cparams(("parallel",)),
        name="inproj_prompt",
    )(x, g, wr, wc, kT_all, vT_all, kiT_all)


def _attn_prompt_body(qT_ref, qiT_ref, wT_ref, kb_ref, kib_ref, vT_ref, bias_ref, cfar_ref, o_ref,
                      key_ref, mask_ref, qz_ref, s_ref, p_ref, m_ref, alpha_ref, acc_ref,
                      *, topk, pos_bits):
    i = pl.program_id(1)
    nsb = i + 1
    nch = (TQ // CH) * (i + 1)

    row_k = lax.broadcasted_iota(I32, (CH, TQ), 0)
    lane_q = lax.broadcasted_iota(I32, (CH, TQ), 1) + i * TQ

    def idx_logits(c, buf):
        kc = kib_ref[pl.ds(pl.multiple_of(c * CH, CH), CH), :]
        for h in range(N_IDX_HEADS):
            s_ref[buf, h] = jnp.dot(kc, qiT_ref[h * IDX_DIM:(h + 1) * IDX_DIM, :],
                                    preferred_element_type=F32)

    def idx_keys(c, buf):
        r0 = pl.multiple_of(c * CH, CH)
        sc = jnp.zeros((CH, TQ), F32)
        for h in range(N_IDX_HEADS):
            sc = sc + wT_ref[h:h + 1, :] * jnp.maximum(s_ref[buf, h], 0.0)
        sc = jnp.where(r0 + row_k <= lane_q, sc, NEG_INF)
        key_ref[pl.ds(r0, CH), :] = _sortable(sc)

    def idx_body(t, carry):
        idx_logits(2 * t, 0)
        idx_logits(2 * t + 1, 1)
        idx_keys(2 * t, 0)
        idx_keys(2 * t + 1, 1)
        return carry

    lax.fori_loop(0, nch // 2, idx_body, 0)

    row_sb = lax.broadcasted_iota(I32, (SB, TQ), 0)

    def count(pred):
        def body(c, acc):
            r0 = pl.multiple_of(c * SB, SB)
            hit = jnp.where(pred(key_ref[pl.ds(r0, SB), :], r0 + row_sb), 1, 0)
            return acc + hit.reshape(SB // SUBLANES, SUBLANES, TQ).sum(axis=0)
        acc = lax.fori_loop(0, nsb, body, jnp.zeros((SUBLANES, TQ), I32))
        return jnp.sum(acc, axis=0, keepdims=True)

    def fill_mask(fn):
        def body(c, carry):
            r0 = pl.multiple_of(c * SB, SB)
            mask_ref[pl.ds(r0, SB), :] = fn(key_ref[pl.ds(r0, SB), :], r0 + row_sb)
            return carry
        lax.fori_loop(0, nsb, body, 0)

    thr = _radix_select(count, topk, (1, TQ))
    need, tie_ok, any_excess = _tie_rule(count, thr, topk)

    @pl.when(jnp.logical_not(any_excess))
    def _():
        thr_eff = jnp.maximum(thr, KEY_NEG_INF + 1)
        fill_mask(lambda kk, pos: jnp.where(kk >= thr_eff, 0.0, NEG_INF))

    @pl.when(any_excess)
    def _():
        last = _last_tie_position(count, thr, need, pos_bits, (1, TQ))
        fill_mask(lambda kk, pos: _tie_mask(kk, pos, thr, last, tie_ok))

    zero_half = jnp.zeros((HEAD_DIM, TQ), BF16)
    for h in range(N_HEADS):
        own = qT_ref[h * HEAD_DIM:(h + 1) * HEAD_DIM, :]
        qz_ref[h] = jnp.concatenate([own, zero_half] if h % 2 == 0 else [zero_half, own], axis=0)
    m_ref[...] = jnp.full(m_ref.shape, NEG_INF, F32)
    acc_ref[...] = jnp.zeros(acc_ref.shape, F32)
    ones_rows = jnp.ones((PV_ROWS - HEAD_DIM, CH), BF16)

    def qk(c, buf):
        r0 = pl.multiple_of(c * CH, CH)
        for h in range(N_HEADS):
            pair = h // 2
            kp = kb_ref[pl.ds(r0, CH), pair * 2 * HEAD_DIM:(pair + 1) * 2 * HEAD_DIM]
            s_ref[buf, h] = jnp.dot(kp, qz_ref[h], preferred_element_type=F32)

    def softmax(c, buf, near):
        msk = mask_ref[pl.ds(pl.multiple_of(c * CH, CH), CH), :]
        for h in range(N_HEADS):
            if near is None:
                s = s_ref[buf, h] + msk
                off = cfar_ref[h:h + 1, :]
            else:
                s = s_ref[buf, h] + bias_ref[near, h] + msk
                off = 0.0
            m_old = m_ref[h:h + 1, :]
            m_new = jnp.maximum(m_old, jnp.max(s, axis=0, keepdims=True) + off)
            m_safe = jnp.where(m_new == NEG_INF, 0.0, m_new)
            alpha_ref[buf, h:h + 1, :] = jnp.exp2(m_old - m_safe)
            p_ref[buf, h] = jnp.exp2(s - (m_safe - off)).astype(BF16)
            m_ref[h:h + 1, :] = m_new

    def pv(c, buf):
        for h in range(N_HEADS):
            vt = jnp.concatenate([vT_ref[c, h * HEAD_DIM:(h + 1) * HEAD_DIM, :], ones_rows], axis=0)
            upd = jnp.dot(vt, p_ref[buf, h], preferred_element_type=F32)
            rows = slice(h * PV_ROWS, (h + 1) * PV_ROWS)
            acc_ref[rows, :] = acc_ref[rows, :] * alpha_ref[buf, h:h + 1, :] + upd

    def chunk_pair(c0, near0, near1):
        qk(c0, 0)
        qk(c0 + 1, 1)
        softmax(c0, 0, near0)
        pv(c0, 0)
        softmax(c0 + 1, 1, near1)
        pv(c0 + 1, 1)

    def far_body(t, carry):
        chunk_pair(2 * t, None, None)
        return carry

    lax.fori_loop(0, jnp.maximum((nch - NEAR) // 2, 0), far_body, 0)

    @pl.when(nch >= NEAR)
    def _():
        chunk_pair(nch - NEAR, NEAR - 1, NEAR - 2)

    chunk_pair(nch - 2, 1, 0)

    for h in range(N_HEADS):
        num = acc_ref[h * PV_ROWS:h * PV_ROWS + HEAD_DIM, :]
        den = acc_ref[h * PV_ROWS + HEAD_DIM:h * PV_ROWS + HEAD_DIM + 1, :]
        o_ref[h * HEAD_DIM:(h + 1) * HEAD_DIM, :] = num / den


def _attn_prompt(qT, qiT, wT, kb, kib, vTc, bias_tab, cfar, batch, seq):
    nq = seq // TQ
    topk = min(TOPK_MAX, seq // 4)
    pos_bits = max(1, (seq - 1).bit_length())
    body = functools.partial(_attn_prompt_body, topk=topk, pos_bits=pos_bits)
    qcol = lambda n: pl.BlockSpec((n, TQ), lambda b, i: (0, b * nq + i))
    return pl.pallas_call(
        body,
        grid=(batch, nq),
        in_specs=[
            qcol(D_ATT), qcol(D_QI), qcol(N_IDX_HEADS),
            pl.BlockSpec((seq, D_ATT), lambda b, i: (b, 0)),
            pl.BlockSpec((seq, IDX_DIM), lambda b, i: (b, 0)),
            pl.BlockSpec((seq // LANES, D_ATT, LANES), lambda b, i: (b, 0, 0)),
            _const_spec(bias_tab.shape), _const_spec(cfar.shape),
        ],
        out_specs=qcol(D_ATT),
        out_shape=jax.ShapeDtypeStruct((D_ATT, batch * seq), F32),
        scratch_shapes=[
            pltpu.VMEM((seq, TQ), I32),
            pltpu.VMEM((seq, TQ), F32),
            pltpu.VMEM((N_HEADS, 2 * HEAD_DIM, TQ), BF16),
            pltpu.VMEM((2, N_HEADS, CH, TQ), F32),
            pltpu.VMEM((2, N_HEADS, CH, TQ), BF16),
            pltpu.VMEM((N_HEADS, TQ), F32),
            pltpu.VMEM((2, N_HEADS, TQ), F32),
            pltpu.VMEM((N_HEADS * PV_ROWS, TQ), F32),
        ],
        compiler_params=_cparams(("parallel", "arbitrary")),
        name="attn_prompt",
    )(qT, qiT, wT, kb, kib, vTc, bias_tab, cfar)


def _merge_tail(x, a_z, p_out, zp, ga, gp, bg_ref, wua_ref, wup_ref, wo_ref):
    a = jnp.dot(a_z.astype(BF16), wua_ref[...], preferred_element_type=F32)
    p = jnp.dot((p_out * _silu(zp)).astype(BF16), wup_ref[...], preferred_element_type=F32)
    mix = _sigmoid(ga + bg_ref[0:1, :]) * a + _sigmoid(gp + bg_ref[1:2, :]) * p
    return x + jnp.dot(mix.astype(BF16), wo_ref[...], preferred_element_type=F32)


def _merge_prompt_body(x_ref, aT_ref, zaT_ref, u_ref, uh_ref, zp_ref, ga_ref, gp_ref,
                       wg_ref, ps_ref, bg_ref, wua_ref, wup_ref, wo_ref, fg_ref,
                       xo_ref, *rest, blocks_per_seq, final):
    ue_ref = rest[-1]
    i = pl.program_id(0)
    blk = i % blocks_per_seq
    ue_ref[0:HIST_PAD, :] = jnp.where(blk == 0, 0.0, uh_ref[...])
    ue_ref[HIST_PAD:HIST_PAD + TM, :] = u_ref[...]
    t_in_seq = blk * TM + lax.broadcasted_iota(I32, (TM, POOL_GROUP), 0)
    parts = []
    for g, w in enumerate(POOL_WINDOWS):
        cols = slice(g * POOL_GROUP, (g + 1) * POOL_GROUP)
        ssum = ue_ref[HIST_PAD:HIST_PAD + TM, cols]
        for j in range(1, w):
            ssum = ssum + ue_ref[HIST_PAD - j:HIST_PAD - j + TM, cols]
        cnt = jnp.minimum(t_in_seq + 1, w).astype(F32)
        diff = ssum / cnt - u_ref[:, cols]
        parts.append(jnp.dot(diff.astype(BF16), wg_ref[g], preferred_element_type=F32))
    p_out = jnp.concatenate(parts, axis=1) * ps_ref[...]
    a_z = (aT_ref[...] * _silu(zaT_ref[...])).T
    xn = _merge_tail(x_ref[...], a_z, p_out, zp_ref[...], ga_ref[...], gp_ref[...],
                     bg_ref, wua_ref, wup_ref, wo_ref)
    if final:
        xo_ref[...] = _rms(xn, fg_ref[...])
    else:
        xo_ref[...] = xn


def _merge_prompt(x, aT, zaT, u, zp, ga, gp, wg, ps, bg, wua, wup, wo, fg, seq, final):
    t, d = x.shape
    d_pool = u.shape[1]
    per = TM // HIST_PAD
    row = lambda n: pl.BlockSpec((TM, n), lambda i: (i, 0))
    col = lambda n: pl.BlockSpec((n, TM), lambda i: (0, i))
    body = functools.partial(_merge_prompt_body, blocks_per_seq=seq // TM, final=final)
    return pl.pallas_call(
        body,
        grid=(t // TM,),
        in_specs=[
            row(d), col(D_ATT), col(D_ATT), row(d_pool),
            pl.BlockSpec((HIST_PAD, d_pool), lambda i: (jnp.maximum(i * per - 1, 0), 0)),
            row(d_pool), row(d), row(d),
            _const_spec(wg.shape), _const_spec(ps.shape), _const_spec(bg.shape),
            _const_spec(wua.shape), _const_spec(wup.shape), _const_spec(wo.shape),
            _const_spec(fg.shape),
        ],
        out_specs=row(d),
        out_shape=jax.ShapeDtypeStruct((t, d), F32),
        scratch_shapes=[pltpu.VMEM((HIST_PAD + TM, d_pool), F32)],
        compiler_params=_cparams(("parallel",)),
        name="merge_prompt",
    )(x, aT, zaT, u, u, zp, ga, gp, wg, ps, bg, wua, wup, wo, fg)


def _merge_sample_body(x_ref, a_ref, ps_ref, ue_ref, wg_ref, psc_ref, bg_ref, wua_ref, wup_ref,
                       wo_ref, fg_ref, xo_ref, *, cols, final):
    db, ext, d_pool = ue_ref.shape
    tn = ext - HIST_PAD
    o_za, o_zp, o_ga, o_gp, d = cols
    parts = []
    for g, w in enumerate(POOL_WINDOWS):
        lanes = slice(g * POOL_GROUP, (g + 1) * POOL_GROUP)
        cur = ue_ref[:, HIST_PAD:HIST_PAD + tn, lanes]
        ssum = cur
        for j in range(1, w):
            ssum = ssum + ue_ref[:, HIST_PAD - j:HIST_PAD - j + tn, lanes]
        diff = (ssum / float(w) - cur).reshape(db * tn, POOL_GROUP)
        parts.append(jnp.dot(diff.astype(BF16), wg_ref[g], preferred_element_type=F32))
    p_out = jnp.concatenate(parts, axis=1) * psc_ref[...]
    a_z = a_ref[...] * _silu(ps_ref[:, o_za:o_za + D_ATT])
    xn = _merge_tail(x_ref[...], a_z, p_out, ps_ref[:, o_zp:o_zp + d_pool],
                     ps_ref[:, o_ga:o_ga + d], ps_ref[:, o_gp:o_gp + d],
                     bg_ref, wua_ref, wup_ref, wo_ref)
    if final:
        xo_ref[...] = _rms(xn, fg_ref[...])
    else:
        xo_ref[...] = xn


def _merge_sample(x, a, ps, ue, wg, psc, bg, wua, wup, wo, fg, cols, final):
    t, d = x.shape
    args = (x, a, ps, ue, wg, psc, bg, wua, wup, wo, fg)
    return pl.pallas_call(
        functools.partial(_merge_sample_body, cols=cols, final=final),
        grid=(1,),
        in_specs=[_const_spec(v.shape) for v in args],
        out_specs=_const_spec((t, d)),
        out_shape=jax.ShapeDtypeStruct((t, d), F32),
        compiler_params=_cparams(("arbitrary",)),
        name="merge_sample",
    )(*args)


def _inproj_sample_body(x_ref, g_ref, w_ref, o_ref):
    hb = _rms(x_ref[...], g_ref[...]).astype(BF16)
    o_ref[...] = _dot_nt(hb, w_ref[...])


def _inproj_sample(x, g, wT, nblk):
    t, d = x.shape
    n = wT.shape[0]
    bn = n // nblk
    return pl.pallas_call(
        _inproj_sample_body,
        grid=(nblk,),
        in_specs=[_const_spec((t, d)), _const_spec((1, d)), pl.BlockSpec((bn, d), lambda j: (j, 0))],
        out_specs=pl.BlockSpec((t, bn), lambda j: (0, j)),
        out_shape=jax.ShapeDtypeStruct((t, n), F32),
        compiler_params=_cparams(("parallel",)),
        name="inproj_sample",
    )(x, g, wT)


def _idx_lhs(qi_blk):
    parts = [qi_blk[:, h * IDX_DIM:(h + 1) * IDX_DIM] for h in range(N_IDX_HEADS)]
    return (jnp.concatenate(parts, axis=0) * IDX_DIM ** -0.5).astype(BF16)


def _idx_scores(lg, kw_blk):
    tn = lg.shape[0] // N_IDX_HEADS
    sc = jnp.zeros((tn, lg.shape[1]), F32)
    for h in range(N_IDX_HEADS):
        w = kw_blk[:, IDX_DIM + h:IDX_DIM + h + 1] * N_IDX_HEADS ** -0.5
        sc = sc + w * jnp.maximum(lg[h * tn:(h + 1) * tn, :], 0.0)
    return sc


def _sidx_body(pt_ref, qi_ref, kw_ref, *rest):
    c_refs, o_ref = rest[:PGI], rest[PGI]
    qi = _idx_lhs(qi_ref[...])
    kw = kw_ref[...]
    for g in range(PGI):
        lg = jnp.dot(qi, c_refs[g][...].astype(BF16), preferred_element_type=F32)
        o_ref[:, g * PAGE:(g + 1) * PAGE] = _idx_scores(lg, kw)


def _sample_idx_scores(page_tab, ps, cidxT, layer, db, tn, n_pages, blk_qi, blk_kw):
    npg = n_pages // PGI

    def page_spec(g):
        return pl.BlockSpec((None, None, IDX_DIM, PAGE),
                            lambda b, j, pt: (layer, pt[b * n_pages + j * PGI + g], 0, 0))

    grid_spec = pltpu.PrefetchScalarGridSpec(
        num_scalar_prefetch=1,
        grid=(db, npg),
        in_specs=[pl.BlockSpec((tn, D_QI), lambda b, j, pt: (b, blk_qi)),
                  pl.BlockSpec((tn, LANES), lambda b, j, pt: (b, blk_kw))]
                 + [page_spec(g) for g in range(PGI)],
        out_specs=pl.BlockSpec((None, tn, PGI * PAGE), lambda b, j, pt: (b, 0, j)),
    )
    return pl.pallas_call(
        _sidx_body,
        grid_spec=grid_spec,
        out_shape=jax.ShapeDtypeStruct((db, tn, n_pages * PAGE), F32),
        compiler_params=_cparams(("parallel", "arbitrary")),
        name="sample_idx",
    )(page_tab, ps, ps, *([cidxT] * PGI))


def _sattn_body(pt_ref, sp_ref, q_ref, kn_ref, vn_ref, qi_ref, kw_ref, bias_ref, *rest,
                topk, pos_bits, n_pages):
    k_refs, v_refs = rest[:PG], rest[PG:2 * PG]
    o_ref, key_ref, mask_ref, qblk_ref, s_ref, m_ref, l_ref, acc_ref = rest[2 * PG:]
    j = pl.program_id(1)
    nchk = n_pages + 1
    tn = o_ref.shape[0]
    rows = N_HEADS * tn

    @pl.when(j == 0)
    def _():
        for c in range(n_pages):
            key_ref[c] = _sortable(sp_ref[:, c * PAGE:(c + 1) * PAGE])
        kw = kw_ref[...]
        kin = jnp.concatenate([kw[:, :IDX_DIM].astype(BF16),
                               jnp.zeros((PAGE - tn, IDX_DIM), BF16)], axis=0)
        sc_new = _idx_scores(_dot_nt(_idx_lhs(qi_ref[...]), kin), kw)
        col = lax.broadcasted_iota(I32, (tn, PAGE), 1)
        row = lax.broadcasted_iota(I32, (tn, PAGE), 0)
        key_ref[n_pages] = _sortable(jnp.where(col <= row, sc_new, NEG_INF))

        def count(pred):
            hits = [jnp.where(pred(key_ref[c], c * PAGE + col), 1, 0) for c in range(nchk)]
            while len(hits) > 1:
                hits = [a + b for a, b in zip(hits[0::2], hits[1::2])] + hits[len(hits) & ~1:]
            return jnp.broadcast_to(jnp.sum(hits[0], axis=1, keepdims=True), (tn, PAGE))

        thr = _radix_select(count, topk, (tn, PAGE))
        need, tie_ok, any_excess = _tie_rule(count, thr, topk)

        @pl.when(jnp.logical_not(any_excess))
        def _():
            thr_eff = jnp.maximum(thr, KEY_NEG_INF + 1)
            for c in range(nchk):
                mask_ref[c] = jnp.where(key_ref[c] >= thr_eff, 0.0, NEG_INF)

        @pl.when(any_excess)
        def _():
            last = _last_tie_position(count, thr, need, pos_bits, (tn, PAGE))
            for c in range(nchk):
                mask_ref[c] = _tie_mask(key_ref[c], c * PAGE + col, thr, last, tie_ok)

        q8 = jnp.concatenate([q_ref[...] * HEAD_DIM ** -0.5] * N_HEADS, axis=0)
        rr = lax.broadcasted_iota(I32, (rows, D_ATT), 0) // tn
        cc = lax.broadcasted_iota(I32, (rows, D_ATT), 1) // HEAD_DIM
        qblk_ref[...] = jnp.where(rr == cc, q8, 0.0).astype(BF16)
        m_ref[...] = jnp.full(m_ref.shape, NEG_INF, F32)
        l_ref[...] = jnp.zeros(l_ref.shape, F32)
        acc_ref[...] = jnp.zeros(acc_ref.shape, F32)

    def softmax_update(n_chunks, pv_fn):
        cm = s_ref[0]
        for g in range(1, n_chunks):
            cm = jnp.maximum(cm, s_ref[g])
        m_old = m_ref[...]
        m_new = jnp.maximum(m_old, jnp.max(cm, axis=1, keepdims=True))
        m_safe = jnp.where(m_new == NEG_INF, 0.0, m_new)
        alpha = jnp.exp(m_old - m_safe)
        psum = jnp.zeros((rows, PAGE), F32)
        pv = jnp.zeros((rows, D_ATT), F32)
        for g in range(n_chunks):
            p = jnp.exp(s_ref[g] - m_safe)
            psum = psum + p
            pv = pv + pv_fn(g, p.astype(BF16))
        l_ref[...] = l_ref[...] * alpha + jnp.sum(psum, axis=1, keepdims=True)
        acc_ref[...] = acc_ref[...] * alpha + pv
        m_ref[...] = m_new

    qblk = qblk_ref[...]
    for g in range(PG):
        pg = j * PG + g
        msk = jnp.concatenate([mask_ref[pg]] * N_HEADS, axis=0)
        jj = jnp.where(pg == n_pages - 1, 1, 0)
        s_ref[g] = (jnp.dot(qblk, k_refs[g][...].astype(BF16), preferred_element_type=F32)
                    + bias_ref[jj] + msk)
    softmax_update(PG, lambda g, p: _dot_nt(p, v_refs[g][...].astype(BF16)))

    @pl.when(j == pl.num_programs(1) - 1)
    def _():
        pad = jnp.zeros((PAGE - tn, D_ATT), BF16)
        kn = jnp.concatenate([kn_ref[...].astype(BF16), pad], axis=0)
        vn = jnp.concatenate([vn_ref[...].astype(BF16), pad], axis=0)
        msk = jnp.concatenate([mask_ref[n_pages]] * N_HEADS, axis=0)
        s_ref[0] = _dot_nt(qblk, kn) + bias_ref[2] + msk
        softmax_update(1, lambda g, p: jnp.dot(p, vn, preferred_element_type=F32))
        out = acc_ref[...] / l_ref[...]
        for h in range(N_HEADS):
            o_ref[:, h * HEAD_DIM:(h + 1) * HEAD_DIM] = out[h * tn:(h + 1) * tn,
                                                          h * HEAD_DIM:(h + 1) * HEAD_DIM]


def _sample_attn(page_tab, sp, ps, bias_s, ckT, cvT, layer, n_pages, blocks):
    db, tn, past = sp.shape
    npg = n_pages // PG
    rows = N_HEADS * tn
    topk = min(TOPK_MAX, (past + tn) // 4)
    pos_bits = (past + PAGE - 1).bit_length()
    body = functools.partial(_sattn_body, topk=topk, pos_bits=pos_bits, n_pages=n_pages)
    blk_q, blk_k, blk_v, blk_qi, blk_kw = blocks

    def page_spec(g):
        return pl.BlockSpec((None, None, D_ATT, PAGE),
                            lambda b, j, pt: (layer, pt[b * n_pages + j * PG + g], 0, 0))

    tok = lambda n, blk: pl.BlockSpec((tn, n), lambda b, j, pt: (b, blk))
    grid_spec = pltpu.PrefetchScalarGridSpec(
        num_scalar_prefetch=1,
        grid=(db, npg),
        in_specs=[pl.BlockSpec((None, tn, past), lambda b, j, pt: (b, 0, 0)),
                  tok(D_ATT, blk_q), tok(D_ATT, blk_k), tok(D_ATT, blk_v), tok(D_QI, blk_qi),
                  tok(LANES, blk_kw),
                  pl.BlockSpec(bias_s.shape, lambda b, j, pt: (0, 0, 0))]
                 + [page_spec(g) for g in range(PG)] * 2,
        out_specs=pl.BlockSpec((tn, D_ATT), lambda b, j, pt: (b, 0)),
        scratch_shapes=[
            pltpu.VMEM((n_pages + 1, tn, PAGE), I32),
            pltpu.VMEM((n_pages + 1, tn, PAGE), F32),
            pltpu.VMEM((rows, D_ATT), BF16),
            pltpu.VMEM((PG, rows, PAGE), F32),
            pltpu.VMEM((rows, 1), F32),
            pltpu.VMEM((rows, 1), F32),
            pltpu.VMEM((rows, D_ATT), F32),
        ],
    )
    return pl.pallas_call(
        body,
        grid_spec=grid_spec,
        out_shape=jax.ShapeDtypeStruct((db * tn, D_ATT), F32),
        compiler_params=_cparams(("parallel", "arbitrary")),
        name="sample_attn",
    )(page_tab, sp, ps, ps, ps, ps, ps, bias_s, *([ckT] * PG), *([cvT] * PG))


def _bias_table(rel_bias, dist):
    onehot = jax.nn.one_hot(_t5_bucket(dist), N_BUCKETS, dtype=F32)
    tab = jnp.einsum("...b,bh->h...", onehot, rel_bias, precision=lax.Precision.HIGHEST)
    return tab


def kernel(x_prompt, x_sample, cache_k, cache_v, cache_idx, state_pool, page_table, rel_bias,
           norm_g, w_in, b_gate, w_pool_group, pool_scale, w_up_attn, w_up_pool, w_out,
           final_norm_g):
    batch, seq, d = x_prompt.shape
    db, tn, _ = x_sample.shape
    depth = w_in.shape[0]
    n_pool = cache_k.shape[1]
    n_pages = page_table.shape[1]
    d_pool = pool_scale.shape[1]
    sizes = (D_ATT, D_ATT, D_ATT, D_ATT, D_QI, IDX_DIM, N_IDX_HEADS, d_pool, d_pool, d, d)
    offs = [int(v) for v in np.concatenate([[0], np.cumsum(sizes)])]
    (o_q, o_k, o_v, o_za, o_qi, o_ki, o_w, o_u, o_zp, o_ga, o_gp) = offs[:-1]
    assert w_in.shape[2] == offs[-1] and seq % TQ == 0 and seq % TM == 0
    assert n_pages % PG == 0 and n_pages % PGI == 0 and tn == SUBLANES
    assert D_ATT == d_pool == 4 * POOL_GROUP and d == 2 * D_ATT

    wT = jnp.swapaxes(w_in, 1, 2).astype(BF16)
    seg = lambda o, n: wT[:, o:o + n, :]
    zrows = lambda n: jnp.zeros((depth, n, d), BF16)
    q_scale = HEAD_DIM ** -0.5 * LOG2E
    w_q2 = (jnp.swapaxes(w_in[:, :, o_q:o_q + D_ATT], 1, 2) * q_scale).astype(BF16)
    w_row = jnp.swapaxes(jnp.concatenate([
        seg(o_k, D_ATT), seg(o_u, d_pool), seg(o_zp, d_pool), seg(o_ga, d), seg(o_gp, d),
        seg(o_ki, IDX_DIM), zrows(LANES - IDX_DIM)], axis=1), 1, 2)
    w_col = jnp.concatenate([
        w_q2, seg(o_k, D_ATT), seg(o_v, D_ATT), seg(o_za, D_ATT),
        seg(o_qi, D_QI) * IDX_DIM ** -0.5, seg(o_ki, IDX_DIM), seg(o_w, N_IDX_HEADS),
        zrows(16 - N_IDX_HEADS)], axis=1)
    s_sizes = (D_ATT, D_ATT, D_ATT, D_ATT, D_QI, d_pool, d_pool, d, d)
    s_offs = [int(v) for v in np.concatenate([[0], np.cumsum(s_sizes)])]
    (so_q, so_k, so_v, so_za, so_qi, so_u, so_zp, so_ga, so_gp, so_kw) = s_offs
    w_smp = jnp.concatenate([
        seg(o_q, D_ATT), seg(o_k, D_ATT), seg(o_v, D_ATT), seg(o_za, D_ATT), seg(o_qi, D_QI),
        seg(o_u, d_pool), seg(o_zp, d_pool), seg(o_ga, d), seg(o_gp, d),
        seg(o_ki, IDX_DIM), seg(o_w, N_IDX_HEADS), zrows(LANES - IDX_DIM - N_IDX_HEADS)], axis=1)
    n_smp = w_smp.shape[1]
    smp_blocks = 5
    assert n_smp % (smp_blocks * LANES) == 0
    wg = w_pool_group.astype(BF16)
    wua = w_up_attn.astype(BF16)
    wup = w_up_pool.astype(BF16)
    wo = w_out.astype(BF16)
    fg = final_norm_g.reshape(1, d)

    r = jnp.arange(CH, dtype=I32)[:, None]
    qq = jnp.arange(TQ, dtype=I32)[None, :]
    dist_p = jnp.stack([qq - r + CH * (jx + 1) - TQ for jx in range(NEAR)])
    bias_p = _bias_table(rel_bias, dist_p) * LOG2E
    bias_p = jnp.swapaxes(bias_p, 0, 1)
    cfar = jnp.broadcast_to((rel_bias[N_BUCKETS - 1] * LOG2E)[:, None], (N_HEADS, TQ))
    tt = jnp.arange(tn, dtype=I32)[:, None]
    ss = jnp.arange(PAGE, dtype=I32)[None, :]
    dist_s = jnp.stack([jnp.full((tn, PAGE), 2 * MAX_DISTANCE, I32), tt + PAGE - ss, tt - ss])
    bias_s = jnp.swapaxes(_bias_table(rel_bias, dist_s), 0, 1).reshape(3, N_HEADS * tn, PAGE)

    ckT = jnp.transpose(cache_k, (0, 1, 3, 4, 2)).reshape(depth, n_pool, D_ATT, PAGE)
    cvT = jnp.transpose(cache_v, (0, 1, 3, 4, 2)).reshape(depth, n_pool, D_ATT, PAGE)
    cidxT = jnp.swapaxes(cache_idx, 2, 3)
    page_flat = page_table.reshape(-1).astype(I32)

    xp = x_prompt.reshape(batch * seq, d)
    xs = x_sample.reshape(db * tn, d)
    kT_all = jnp.zeros((depth, batch, D_ATT, seq), F32)
    vT_all = jnp.zeros((depth, batch, D_ATT, seq), F32)
    kiT_all = jnp.zeros((depth, batch, IDX_DIM, seq), F32)
    pool_p, k_smp, v_smp, ki_smp, pool_s = [], [], [], [], []
    for l in range(depth):
        final = l == depth - 1
        g_l = norm_g[l].reshape(1, d)
        psc = pool_scale[l].reshape(1, d_pool)
        (kT_all, vT_all, kiT_all, u, zp, ga, gp, kb, kib, qT, vTc, zaT, qiT, wTi) = _inproj_prompt(
            xp, g_l, w_row[l], w_col[l], kT_all, vT_all, kiT_all, l, batch, seq, d_pool)
        aT = _attn_prompt(qT, qiT, wTi, kb, kib, vTc, bias_p, cfar, batch, seq)
        xp = _merge_prompt(xp, aT, zaT, u, zp, ga, gp, wg[l], psc, b_gate[l], wua[l], wup[l],
                           wo[l], fg, seq, final)
        pool_p.append(u.reshape(batch, seq, d_pool)[:, seq - POOL_HIST:])
        ps = _inproj_sample(xs, g_l, w_smp[l], smp_blocks)
        sp = _sample_idx_scores(page_flat, ps, cidxT, l, db, tn, n_pages,
                                so_qi // D_QI, so_kw // LANES)
        a_s = _sample_attn(page_flat, sp, ps, bias_s, ckT, cvT, l, n_pages,
                           (so_q // D_ATT, so_k // D_ATT, so_v // D_ATT, so_qi // D_QI,
                            so_kw // LANES))
        u_s = ps[:, so_u:so_u + d_pool].reshape(db, tn, d_pool)
        u_ext = jnp.concatenate([state_pool[l], u_s], axis=1)
        ue = jnp.pad(u_ext, ((0, 0), (HIST_PAD - POOL_HIST, 0), (0, 0)))
        xs = _merge_sample(xs, a_s, ps, ue, wg[l], psc, b_gate[l], wua[l], wup[l], wo[l], fg,
                           (so_za, so_zp, so_ga, so_gp, d), final)
        k_smp.append(ps[:, so_k:so_k + D_ATT].reshape(db, tn, N_HEADS, HEAD_DIM))
        v_smp.append(ps[:, so_v:so_v + D_ATT].reshape(db, tn, N_HEADS, HEAD_DIM))
        ki_smp.append(ps[:, so_kw:so_kw + IDX_DIM].reshape(db, tn, IDX_DIM))
        pool_s.append(u_ext[:, tn:])
    heads_last = lambda a: jnp.transpose(a.reshape(depth, batch, N_HEADS, HEAD_DIM, seq),
                                         (0, 1, 4, 2, 3))
    return (xp.reshape(batch, seq, d), xs.reshape(db, tn, d),
            heads_last(kT_all), heads_last(vT_all), jnp.swapaxes(kiT_all, 2, 3),
            jnp.stack(pool_p), jnp.stack(k_smp), jnp.stack(v_smp), jnp.stack(ki_smp),
            jnp.stack(pool_s))
```

```python
import functools

import numpy as np
import jax
import jax.numpy as jnp
from jax import lax
from jax.experimental import pallas as pl
from jax.experimental.pallas import tpu as pltpu

F32 = jnp.float32
BF16 = jnp.bfloat16
I32 = jnp.int32
I16 = jnp.int16
I16_MIN = -32768

N_HEADS = 8
HEAD_DIM = 64
D_ATT = N_HEADS * HEAD_DIM
N_IDX_HEADS = 8
IDX_DIM = 64
D_QI = N_IDX_HEADS * IDX_DIM
TOPK_MAX = 256
PAGE = 128
POOL_WINDOWS = (2, 4, 8, 16)
POOL_GROUP = 128
POOL_HIST = 15
N_BUCKETS = 32
MAX_DISTANCE = 128
RMS_EPS = 1e-6
LOG2E = 1.4426950408889634

LANES = 128
SUBLANES = 8
HALF_ROWS = 16
VMEM_LIMIT = 56 * 1024 * 1024

TQ = 256
CH = 128
SB = 256
TM = 256
NEAR = 2 + TQ // CH
PV_ROWS = HEAD_DIM + 16
PG = 8
PGI = 32
HIST_PAD = 16

NEG_INF = float("-inf")
INT_MIN = np.int32(-2 ** 31)
KEY_NEG_INF = np.int32(-0x7F800000)


def _sortable(x):
    b = lax.bitcast_convert_type(x, I32)
    return jnp.where(b < 0, INT_MIN - b, b)


def _sigmoid(x):
    return 1.0 / (1.0 + jnp.exp(-x))


def _silu(x):
    return x * _sigmoid(x)


def _rms(x, g):
    ms = jnp.mean(x * x, axis=-1, keepdims=True)
    return (x * lax.rsqrt(ms + RMS_EPS)) * g


def _t5_bucket(dist):
    n = jnp.maximum(dist, 0)
    max_exact = N_BUCKETS // 2
    nf = jnp.maximum(n, 1).astype(F32)
    large = max_exact + (jnp.log(nf / max_exact) / np.log(MAX_DISTANCE / max_exact)
                         * (N_BUCKETS - max_exact)).astype(I32)
    large = jnp.minimum(large, N_BUCKETS - 1)
    return jnp.where(n < max_exact, n, large)


def _dot_nt(a, b):
    return lax.dot_general(a, b, (((1,), (1,)), ((), ())), preferred_element_type=F32)


def _cparams(sem):
    return pltpu.CompilerParams(dimension_semantics=sem, vmem_limit_bytes=VMEM_LIMIT)


def _const_spec(shape):
    nd = len(shape)
    return pl.BlockSpec(shape, lambda *_: (0,) * nd)


def _radix_select(count, topk, shape):
    zeros = jnp.zeros(shape, I32)
    t0 = jnp.where(count(lambda kk, pos: kk >= zeros) >= topk, zeros, jnp.full(shape, INT_MIN, I32))

    def bit_body(s, t):
        cand = t | jnp.left_shift(jnp.int32(1), 30 - s)
        return jnp.where(count(lambda kk, pos: kk >= cand) >= topk, cand, t)

    return lax.fori_loop(0, 31, bit_body, t0)


def _tie_rule(count, thr, topk):
    need = topk - count(lambda kk, pos: kk > thr)
    tie_ok = thr > KEY_NEG_INF
    excess = jnp.logical_and(count(lambda kk, pos: kk == thr) > need, tie_ok)
    return need, tie_ok, jnp.max(jnp.where(excess, 1, 0)) > 0


def _last_tie_position(count, thr, need, pos_bits, shape):
    def pos_body(s, p):
        cand = p + jnp.left_shift(jnp.int32(1), pos_bits - 1 - s)
        cnt = count(lambda kk, pos: jnp.logical_and(kk == thr, pos < cand))
        return jnp.where(cnt <= need - 1, cand, p)
    return lax.fori_loop(0, pos_bits, pos_body, jnp.zeros(shape, I32))


def _tie_mask(kk, pos, thr, last, tie_ok):
    tie = jnp.logical_and(jnp.logical_and(kk == thr, pos <= last), tie_ok)
    sel = jnp.logical_and(jnp.logical_or(kk > thr, tie), kk > KEY_NEG_INF)
    return jnp.where(sel, 0.0, NEG_INF)


def _inproj_prompt_body(x_ref, g_ref, wr_ref, wc_ref, kT_in, vT_in, kiT_in,
                        kT_ref, vT_ref, kiT_ref, u_ref, zp_ref, ga_ref, gp_ref,
                        kb_ref, kib_ref, qT_ref, vTc_ref, zaT_ref, qiT_ref, wT_ref):
    del kT_in, vT_in, kiT_in
    d_pool = u_ref.shape[1]
    d_model = ga_ref.shape[1]
    hb = _rms(x_ref[...], g_ref[...]).astype(BF16)

    def rowdot(a, n):
        return jnp.dot(hb, wr_ref[:, a:a + n], preferred_element_type=F32)

    def coldot(a, n):
        return _dot_nt(wc_ref[a:a + n, :], hb)

    o = 0
    kb_ref[...] = rowdot(o, D_ATT).astype(BF16); o += D_ATT
    u_ref[...] = rowdot(o, d_pool); o += d_pool
    zp_ref[...] = rowdot(o, d_pool); o += d_pool
    ga_ref[...] = rowdot(o, d_model); o += d_model
    gp_ref[...] = rowdot(o, d_model); o += d_model
    kib_ref[...] = rowdot(o, LANES)[:, :IDX_DIM].astype(BF16)

    o = 0
    qT_ref[...] = coldot(o, D_ATT).astype(BF16); o += D_ATT
    kT_ref[...] = coldot(o, D_ATT); o += D_ATT
    vT = coldot(o, D_ATT); o += D_ATT
    vT_ref[...] = vT
    vTb = vT.astype(BF16)
    for j in range(vTc_ref.shape[0]):
        vTc_ref[j] = vTb[:, j * LANES:(j + 1) * LANES]
    zaT_ref[...] = coldot(o, D_ATT); o += D_ATT
    qiT_ref[...] = coldot(o, D_QI).astype(BF16); o += D_QI
    kiT_ref[...] = coldot(o, IDX_DIM); o += IDX_DIM
    wT_ref[...] = coldot(o, 16)[0:N_IDX_HEADS] * (N_IDX_HEADS ** -0.5)


def _inproj_prompt(x, g, wr, wc, kT_all, vT_all, kiT_all, layer, batch, seq, d_pool):
    t, d = x.shape
    per = seq // TM
    row = lambda n: pl.BlockSpec((TM, n), lambda i: (i, 0))
    col = lambda n: pl.BlockSpec((n, TM), lambda i: (0, i))
    stacked = lambda n: pl.BlockSpec((None, None, n, TM), lambda i: (layer, i // per, 0, i % per))
    hbm = pl.BlockSpec(memory_space=pl.ANY)
    sds = jax.ShapeDtypeStruct
    out_shape = (
        sds(kT_all.shape, F32), sds(vT_all.shape, F32), sds(kiT_all.shape, F32),
        sds((t, d_pool), F32),
        sds((t, d_pool), F32),
        sds((t, d), F32),
        sds((t, d), F32),
        sds((t, D_ATT), BF16),
        sds((t, IDX_DIM), BF16),
        sds((D_ATT, t), BF16),
        sds((t // LANES, D_ATT, LANES), BF16),
        sds((D_ATT, t), F32),
        sds((D_QI, t), BF16),
        sds((N_IDX_HEADS, t), F32),
    )
    out_specs = (
        stacked(D_ATT), stacked(D_ATT), stacked(IDX_DIM),
        row(d_pool), row(d_pool), row(d), row(d), row(D_ATT), row(IDX_DIM), col(D_ATT),
        pl.BlockSpec((TM // LANES, D_ATT, LANES), lambda i: (i, 0, 0)),
        col(D_ATT), col(D_QI), col(N_IDX_HEADS),
    )
    return pl.pallas_call(
        _inproj_prompt_body,
        grid=(t // TM,),
        in_specs=[row(d), _const_spec((1, d)), _const_spec(wr.shape), _const_spec(wc.shape),
                  hbm, hbm, hbm],
        out_specs=out_specs,
        out_shape=out_shape,
        input_output_aliases={4: 0, 5: 1, 6: 2},
        compiler_params=_cparams(("parallel",)),
        name="inproj_prompt",
    )(x, g, wr, wc, kT_all, vT_all, kiT_all)


def _attn_prompt_body(qT_ref, qiT_ref, wT_ref, kb_ref, kib_ref, vT_ref, bias_ref, cfar_ref, o_ref,
                      key_ref, hi_ref, lo_ref, y_ref, mask_ref, qz_ref, s_ref, p_ref, m_ref,
                      alpha_ref, acc_ref,
                      *, topk, pos_bits):
    i = pl.program_id(1)
    nsb = i + 1
    nch = (TQ // CH) * (i + 1)

    row_k = lax.broadcasted_iota(I32, (CH, TQ), 0)
    lane_q = lax.broadcasted_iota(I32, (CH, TQ), 1) + i * TQ

    def idx_logits(c, buf):
        kc = kib_ref[pl.ds(pl.multiple_of(c * CH, CH), CH), :]
        for h in range(N_IDX_HEADS):
            s_ref[buf, h] = jnp.dot(kc, qiT_ref[h * IDX_DIM:(h + 1) * IDX_DIM, :],
                                    preferred_element_type=F32)

    def idx_keys(c, buf):
        r0 = pl.multiple_of(c * CH, CH)
        sc = jnp.zeros((CH, TQ), F32)
        for h in range(N_IDX_HEADS):
            sc = sc + wT_ref[h:h + 1, :] * jnp.maximum(s_ref[buf, h], 0.0)
        sc = jnp.where(r0 + row_k <= lane_q, sc, NEG_INF)
        key = _sortable(sc)
        key_ref[pl.ds(r0, CH), :] = key
        hi_ref[pl.ds(r0, CH), :] = jnp.right_shift(key, 16).astype(I16)
        lo_ref[pl.ds(r0, CH), :] = ((key & 0xFFFF) + I16_MIN).astype(I16)

    def idx_body(t, carry):
        idx_logits(2 * t, 0)
        idx_logits(2 * t + 1, 1)
        idx_keys(2 * t, 0)
        idx_keys(2 * t + 1, 1)
        return carry

    lax.fori_loop(0, nch // 2, idx_body, 0)

    row_sb = lax.broadcasted_iota(I32, (SB, TQ), 0)

    def count(pred):
        def body(c, acc):
            r0 = pl.multiple_of(c * SB, SB)
            hit = jnp.where(pred(key_ref[pl.ds(r0, SB), :], r0 + row_sb), 1, 0)
            return acc + hit.reshape(SB // SUBLANES, SUBLANES, TQ).sum(axis=0)
        acc = lax.fori_loop(0, nsb, body, jnp.zeros((SUBLANES, TQ), I32))
        return jnp.sum(acc, axis=0, keepdims=True)

    def fill_mask(fn):
        def body(c, carry):
            r0 = pl.multiple_of(c * SB, SB)
            mask_ref[pl.ds(r0, SB), :] = fn(key_ref[pl.ds(r0, SB), :], r0 + row_sb)
            return carry
        lax.fori_loop(0, nsb, body, 0)

    groups = SB // HALF_ROWS
    one16 = jnp.ones((groups, HALF_ROWS, TQ), I16)
    zero16 = jnp.zeros((groups, HALF_ROWS, TQ), I16)

    def count16(ref, pred):
        def body(c, acc):
            r0 = pl.multiple_of(c * SB, SB)
            blk = ref[pl.ds(r0, SB), :].reshape(groups, HALF_ROWS, TQ)
            hit = jnp.where(pred(blk), one16, zero16)
            parts = [hit[r] for r in range(groups)]
            while len(parts) > 1:
                parts = [a + b for a, b in zip(parts[0::2], parts[1::2])]
            return acc + parts[0]
        acc = lax.fori_loop(0, nsb, body, jnp.zeros((HALF_ROWS, TQ), I16))
        tot = jnp.sum(acc.astype(I32), axis=0, keepdims=True)
        return jnp.broadcast_to(tot, (HALF_ROWS, TQ))

    def select16(ref, target):
        def ge(cand32):
            c16 = cand32.astype(I16)[None]
            return count16(ref, lambda blk: blk >= c16)
        zeros = jnp.zeros((HALF_ROWS, TQ), I32)
        t0 = jnp.where(ge(zeros) >= target, zeros, jnp.full((HALF_ROWS, TQ), I16_MIN, I32))

        def bit_body(s, t):
            cand = t | jnp.left_shift(jnp.int32(1), 14 - s)
            return jnp.where(ge(cand) >= target, cand, t)
        return lax.fori_loop(0, 15, bit_body, t0)

    thr_hi = select16(hi_ref, topk)
    thr_hi16 = thr_hi.astype(I16)[None]
    need_lo = topk - count16(hi_ref, lambda blk: blk > thr_hi16)

    def y_body(c, carry):
        r0 = pl.multiple_of(c * SB, SB)
        hi = hi_ref[pl.ds(r0, SB), :].reshape(groups, HALF_ROWS, TQ)
        lo = lo_ref[pl.ds(r0, SB), :].reshape(groups, HALF_ROWS, TQ)
        y = jnp.where(hi == thr_hi16, lo, jnp.full(lo.shape, I16_MIN, I16))
        y_ref[pl.ds(r0, SB), :] = y.reshape(SB, TQ)
        return carry
    lax.fori_loop(0, nsb, y_body, 0)
    thr_lo = select16(y_ref, need_lo)
    thr = (thr_hi[0:1] * 65536) | (thr_lo[0:1] - I16_MIN)
    need, tie_ok, any_excess = _tie_rule(count, thr, topk)

    @pl.when(jnp.logical_not(any_excess))
    def _():
        thr_eff = jnp.maximum(thr, KEY_NEG_INF + 1)
        fill_mask(lambda kk, pos: jnp.where(kk >= thr_eff, 0.0, NEG_INF))

    @pl.when(any_excess)
    def _():
        last = _last_tie_position(count, thr, need, pos_bits, (1, TQ))
        fill_mask(lambda kk, pos: _tie_mask(kk, pos, thr, last, tie_ok))

    zero_half = jnp.zeros((HEAD_DIM, TQ), BF16)
    for h in range(N_HEADS):
        own = qT_ref[h * HEAD_DIM:(h + 1) * HEAD_DIM, :]
        qz_ref[h] = jnp.concatenate([own, zero_half] if h % 2 == 0 else [zero_half, own], axis=0)
    m_ref[...] = jnp.full(m_ref.shape, NEG_INF, F32)
    acc_ref[...] = jnp.zeros(acc_ref.shape, F32)
    ones_rows = jnp.ones((PV_ROWS - HEAD_DIM, CH), BF16)

    def qk(c, buf):
        r0 = pl.multiple_of(c * CH, CH)
        for h in range(N_HEADS):
            pair = h // 2
            kp = kb_ref[pl.ds(r0, CH), pair * 2 * HEAD_DIM:(pair + 1) * 2 * HEAD_DIM]
            s_ref[buf, h] = jnp.dot(kp, qz_ref[h], preferred_element_type=F32)

    def softmax(c, buf, near):
        msk = mask_ref[pl.ds(pl.multiple_of(c * CH, CH), CH), :]
        for h in range(N_HEADS):
            if near is None:
                s = s_ref[buf, h] + msk
                off = cfar_ref[h:h + 1, :]
            else:
                s = s_ref[buf, h] + bias_ref[near, h] + msk
                off = 0.0
            m_old = m_ref[h:h + 1, :]
            m_new = jnp.maximum(m_old, jnp.max(s, axis=0, keepdims=True) + off)
            m_safe = jnp.where(m_new == NEG_INF, 0.0, m_new)
            alpha_ref[buf, h:h + 1, :] = jnp.exp2(m_old - m_safe)
            p_ref[buf, h] = jnp.exp2(s - (m_safe - off)).astype(BF16)
            m_ref[h:h + 1, :] = m_new

    def pv(c, buf):
        for h in range(N_HEADS):
            vt = jnp.concatenate([vT_ref[c, h * HEAD_DIM:(h + 1) * HEAD_DIM, :], ones_rows], axis=0)
            upd = jnp.dot(vt, p_ref[buf, h], preferred_element_type=F32)
            rows = slice(h * PV_ROWS, (h + 1) * PV_ROWS)
            acc_ref[rows, :] = acc_ref[rows, :] * alpha_ref[buf, h:h + 1, :] + upd

    def chunk_pair(c0, near0, near1):
        qk(c0, 0)
        qk(c0 + 1, 1)
        softmax(c0, 0, near0)
        pv(c0, 0)
        softmax(c0 + 1, 1, near1)
        pv(c0 + 1, 1)

    def far_body(t, carry):
        chunk_pair(2 * t, None, None)
        return carry

    lax.fori_loop(0, jnp.maximum((nch - NEAR) // 2, 0), far_body, 0)

    @pl.when(nch >= NEAR)
    def _():
        chunk_pair(nch - NEAR, NEAR - 1, NEAR - 2)

    chunk_pair(nch - 2, 1, 0)

    for h in range(N_HEADS):
        num = acc_ref[h * PV_ROWS:h * PV_ROWS + HEAD_DIM, :]
        den = acc_ref[h * PV_ROWS + HEAD_DIM:h * PV_ROWS + HEAD_DIM + 1, :]
        o_ref[h * HEAD_DIM:(h + 1) * HEAD_DIM, :] = num / den


def _attn_prompt(qT, qiT, wT, kb, kib, vTc, bias_tab, cfar, batch, seq):
    nq = seq // TQ
    topk = min(TOPK_MAX, seq // 4)
    pos_bits = max(1, (seq - 1).bit_length())
    body = functools.partial(_attn_prompt_body, topk=topk, pos_bits=pos_bits)
    qcol = lambda n: pl.BlockSpec((n, TQ), lambda b, i: (0, b * nq + i))
    return pl.pallas_call(
        body,
        grid=(batch, nq),
        in_specs=[
            qcol(D_ATT), qcol(D_QI), qcol(N_IDX_HEADS),
            pl.BlockSpec((seq, D_ATT), lambda b, i: (b, 0)),
            pl.BlockSpec((seq, IDX_DIM), lambda b, i: (b, 0)),
            pl.BlockSpec((seq // LANES, D_ATT, LANES), lambda b, i: (b, 0, 0)),
            _const_spec(bias_tab.shape), _const_spec(cfar.shape),
        ],
        out_specs=qcol(D_ATT),
        out_shape=jax.ShapeDtypeStruct((D_ATT, batch * seq), F32),
        scratch_shapes=[
            pltpu.VMEM((seq, TQ), I32),
            pltpu.VMEM((seq, TQ), I16),
            pltpu.VMEM((seq, TQ), I16),
            pltpu.VMEM((seq, TQ), I16),
            pltpu.VMEM((seq, TQ), F32),
            pltpu.VMEM((N_HEADS, 2 * HEAD_DIM, TQ), BF16),
            pltpu.VMEM((2, N_HEADS, CH, TQ), F32),
            pltpu.VMEM((2, N_HEADS, CH, TQ), BF16),
            pltpu.VMEM((N_HEADS, TQ), F32),
            pltpu.VMEM((2, N_HEADS, TQ), F32),
            pltpu.VMEM((N_HEADS * PV_ROWS, TQ), F32),
        ],
        compiler_params=_cparams(("parallel", "arbitrary")),
        name="attn_prompt",
    )(qT, qiT, wT, kb, kib, vTc, bias_tab, cfar)


def _merge_tail(x, a_z, p_out, zp, ga, gp, bg_ref, wua_ref, wup_ref, wo_ref):
    a = jnp.dot(a_z.astype(BF16), wua_ref[...], preferred_element_type=F32)
    p = jnp.dot((p_out * _silu(zp)).astype(BF16), wup_ref[...], preferred_element_type=F32)
    mix = _sigmoid(ga + bg_ref[0:1, :]) * a + _sigmoid(gp + bg_ref[1:2, :]) * p
    return x + jnp.dot(mix.astype(BF16), wo_ref[...], preferred_element_type=F32)


def _merge_prompt_body(x_ref, aT_ref, zaT_ref, u_ref, uh_ref, zp_ref, ga_ref, gp_ref,
                       wg_ref, ps_ref, bg_ref, wua_ref, wup_ref, wo_ref, fg_ref,
                       xo_ref, *rest, blocks_per_seq, final):
    ue_ref = rest[-1]
    i = pl.program_id(0)
    blk = i % blocks_per_seq
    ue_ref[0:HIST_PAD, :] = jnp.where(blk == 0, 0.0, uh_ref[...])
    ue_ref[HIST_PAD:HIST_PAD + TM, :] = u_ref[...]
    t_in_seq = blk * TM + lax.broadcasted_iota(I32, (TM, POOL_GROUP), 0)
    parts = []
    for g, w in enumerate(POOL_WINDOWS):
        cols = slice(g * POOL_GROUP, (g + 1) * POOL_GROUP)
        ssum = ue_ref[HIST_PAD:HIST_PAD + TM, cols]
        for j in range(1, w):
            ssum = ssum + ue_ref[HIST_PAD - j:HIST_PAD - j + TM, cols]
        cnt = jnp.minimum(t_in_seq + 1, w).astype(F32)
        diff = ssum / cnt - u_ref[:, cols]
        parts.append(jnp.dot(diff.astype(BF16), wg_ref[g], preferred_element_type=F32))
    p_out = jnp.concatenate(parts, axis=1) * ps_ref[...]
    a_z = (aT_ref[...] * _silu(zaT_ref[...])).T
    xn = _merge_tail(x_ref[...], a_z, p_out, zp_ref[...], ga_ref[...], gp_ref[...],
                     bg_ref, wua_ref, wup_ref, wo_ref)
    if final:
        xo_ref[...] = _rms(xn, fg_ref[...])
    else:
        xo_ref[...] = xn


def _merge_prompt(x, aT, zaT, u, zp, ga, gp, wg, ps, bg, wua, wup, wo, fg, seq, final):
    t, d = x.shape
    d_pool = u.shape[1]
    per = TM // HIST_PAD
    row = lambda n: pl.BlockSpec((TM, n), lambda i: (i, 0))
    col = lambda n: pl.BlockSpec((n, TM), lambda i: (0, i))
    body = functools.partial(_merge_prompt_body, blocks_per_seq=seq // TM, final=final)
    return pl.pallas_call(
        body,
        grid=(t // TM,),
        in_specs=[
            row(d), col(D_ATT), col(D_ATT), row(d_pool),
            pl.BlockSpec((HIST_PAD, d_pool), lambda i: (jnp.maximum(i * per - 1, 0), 0)),
            row(d_pool), row(d), row(d),
            _const_spec(wg.shape), _const_spec(ps.shape), _const_spec(bg.shape),
            _const_spec(wua.shape), _const_spec(wup.shape), _const_spec(wo.shape),
            _const_spec(fg.shape),
        ],
        out_specs=row(d),
        out_shape=jax.ShapeDtypeStruct((t, d), F32),
        scratch_shapes=[pltpu.VMEM((HIST_PAD + TM, d_pool), F32)],
        compiler_params=_cparams(("parallel",)),
        name="merge_prompt",
    )(x, aT, zaT, u, u, zp, ga, gp, wg, ps, bg, wua, wup, wo, fg)


def _merge_sample_body(x_ref, a_ref, ps_ref, ue_ref, wg_ref, psc_ref, bg_ref, wua_ref, wup_ref,
                       wo_ref, fg_ref, xo_ref, *, cols, final):
    db, ext, d_pool = ue_ref.shape
    tn = ext - HIST_PAD
    o_za, o_zp, o_ga, o_gp, d = cols
    parts = []
    for g, w in enumerate(POOL_WINDOWS):
        lanes = slice(g * POOL_GROUP, (g + 1) * POOL_GROUP)
        cur = ue_ref[:, HIST_PAD:HIST_PAD + tn, lanes]
        ssum = cur
        for j in range(1, w):
            ssum = ssum + ue_ref[:, HIST_PAD - j:HIST_PAD - j + tn, lanes]
        diff = (ssum / float(w) - cur).reshape(db * tn, POOL_GROUP)
        parts.append(jnp.dot(diff.astype(BF16), wg_ref[g], preferred_element_type=F32))
    p_out = jnp.concatenate(parts, axis=1) * psc_ref[...]
    a_z = a_ref[...] * _silu(ps_ref[:, o_za:o_za + D_ATT])
    xn = _merge_tail(x_ref[...], a_z, p_out, ps_ref[:, o_zp:o_zp + d_pool],
                     ps_ref[:, o_ga:o_ga + d], ps_ref[:, o_gp:o_gp + d],
                     bg_ref, wua_ref, wup_ref, wo_ref)
    if final:
        xo_ref[...] = _rms(xn, fg_ref[...])
    else:
        xo_ref[...] = xn


def _merge_sample(x, a, ps, ue, wg, psc, bg, wua, wup, wo, fg, cols, final):
    t, d = x.shape
    args = (x, a, ps, ue, wg, psc, bg, wua, wup, wo, fg)
    return pl.pallas_call(
        functools.partial(_merge_sample_body, cols=cols, final=final),
        grid=(1,),
        in_specs=[_const_spec(v.shape) for v in args],
        out_specs=_const_spec((t, d)),
        out_shape=jax.ShapeDtypeStruct((t, d), F32),
        compiler_params=_cparams(("arbitrary",)),
        name="merge_sample",
    )(*args)


def _inproj_sample_body(x_ref, g_ref, w_ref, o_ref):
    hb = _rms(x_ref[...], g_ref[...]).astype(BF16)
    o_ref[...] = _dot_nt(hb, w_ref[...])


def _inproj_sample(x, g, wT, nblk):
    t, d = x.shape
    n = wT.shape[0]
    bn = n // nblk
    return pl.pallas_call(
        _inproj_sample_body,
        grid=(nblk,),
        in_specs=[_const_spec((t, d)), _const_spec((1, d)), pl.BlockSpec((bn, d), lambda j: (j, 0))],
        out_specs=pl.BlockSpec((t, bn), lambda j: (0, j)),
        out_shape=jax.ShapeDtypeStruct((t, n), F32),
        compiler_params=_cparams(("parallel",)),
        name="inproj_sample",
    )(x, g, wT)


def _idx_lhs(qi_blk):
    parts = [qi_blk[:, h * IDX_DIM:(h + 1) * IDX_DIM] for h in range(N_IDX_HEADS)]
    return (jnp.concatenate(parts, axis=0) * IDX_DIM ** -0.5).astype(BF16)


def _idx_scores(lg, kw_blk):
    tn = lg.shape[0] // N_IDX_HEADS
    sc = jnp.zeros((tn, lg.shape[1]), F32)
    for h in range(N_IDX_HEADS):
        w = kw_blk[:, IDX_DIM + h:IDX_DIM + h + 1] * N_IDX_HEADS ** -0.5
        sc = sc + w * jnp.maximum(lg[h * tn:(h + 1) * tn, :], 0.0)
    return sc


def _sidx_body(pt_ref, qi_ref, kw_ref, *rest):
    c_refs, o_ref = rest[:PGI], rest[PGI]
    qi = _idx_lhs(qi_ref[...])
    kw = kw_ref[...]
    for g in range(PGI):
        lg = jnp.dot(qi, c_refs[g][...].astype(BF16), preferred_element_type=F32)
        o_ref[:, g * PAGE:(g + 1) * PAGE] = _idx_scores(lg, kw)


def _sample_idx_scores(page_tab, ps, cidxT, layer, db, tn, n_pages, blk_qi, blk_kw):
    npg = n_pages // PGI

    def page_spec(g):
        return pl.BlockSpec((None, None, IDX_DIM, PAGE),
                            lambda b, j, pt: (layer, pt[b * n_pages + j * PGI + g], 0, 0))

    grid_spec = pltpu.PrefetchScalarGridSpec(
        num_scalar_prefetch=1,
        grid=(db, npg),
        in_specs=[pl.BlockSpec((tn, D_QI), lambda b, j, pt: (b, blk_qi)),
                  pl.BlockSpec((tn, LANES), lambda b, j, pt: (b, blk_kw))]
                 + [page_spec(g) for g in range(PGI)],
        out_specs=pl.BlockSpec((None, tn, PGI * PAGE), lambda b, j, pt: (b, 0, j)),
    )
    return pl.pallas_call(
        _sidx_body,
        grid_spec=grid_spec,
        out_shape=jax.ShapeDtypeStruct((db, tn, n_pages * PAGE), F32),
        compiler_params=_cparams(("parallel", "arbitrary")),
        name="sample_idx",
    )(page_tab, ps, ps, *([cidxT] * PGI))


def _sattn_body(pt_ref, sp_ref, q_ref, kn_ref, vn_ref, qi_ref, kw_ref, bias_ref, *rest,
                topk, pos_bits, n_pages):
    k_refs, v_refs = rest[:PG], rest[PG:2 * PG]
    o_ref, key_ref, mask_ref, qblk_ref, s_ref, m_ref, l_ref, acc_ref = rest[2 * PG:]
    j = pl.program_id(1)
    nchk = n_pages + 1
    tn = o_ref.shape[0]
    rows = N_HEADS * tn

    @pl.when(j == 0)
    def _():
        for c in range(n_pages):
            key_ref[c] = _sortable(sp_ref[:, c * PAGE:(c + 1) * PAGE])
        kw = kw_ref[...]
        kin = jnp.concatenate([kw[:, :IDX_DIM].astype(BF16),
                               jnp.zeros((PAGE - tn, IDX_DIM), BF16)], axis=0)
        sc_new = _idx_scores(_dot_nt(_idx_lhs(qi_ref[...]), kin), kw)
        col = lax.broadcasted_iota(I32, (tn, PAGE), 1)
        row = lax.broadcasted_iota(I32, (tn, PAGE), 0)
        key_ref[n_pages] = _sortable(jnp.where(col <= row, sc_new, NEG_INF))

        def count(pred):
            hits = [jnp.where(pred(key_ref[c], c * PAGE + col), 1, 0) for c in range(nchk)]
            while len(hits) > 1:
                hits = [a + b for a, b in zip(hits[0::2], hits[1::2])] + hits[len(hits) & ~1:]
            return jnp.broadcast_to(jnp.sum(hits[0], axis=1, keepdims=True), (tn, PAGE))

        thr = _radix_select(count, topk, (tn, PAGE))
        need, tie_ok, any_excess = _tie_rule(count, thr, topk)

        @pl.when(jnp.logical_not(any_excess))
        def _():
            thr_eff = jnp.maximum(thr, KEY_NEG_INF + 1)
            for c in range(nchk):
                mask_ref[c] = jnp.where(key_ref[c] >= thr_eff, 0.0, NEG_INF)

        @pl.when(any_excess)
        def _():
            last = _last_tie_position(count, thr, need, pos_bits, (tn, PAGE))
            for c in range(nchk):
                mask_ref[c] = _tie_mask(key_ref[c], c * PAGE + col, thr, last, tie_ok)

        q8 = jnp.concatenate([q_ref[...] * HEAD_DIM ** -0.5] * N_HEADS, axis=0)
        rr = lax.broadcasted_iota(I32, (rows, D_ATT), 0) // tn
        cc = lax.broadcasted_iota(I32, (rows, D_ATT), 1) // HEAD_DIM
        qblk_ref[...] = jnp.where(rr == cc, q8, 0.0).astype(BF16)
        m_ref[...] = jnp.full(m_ref.shape, NEG_INF, F32)
        l_ref[...] = jnp.zeros(l_ref.shape, F32)
        acc_ref[...] = jnp.zeros(acc_ref.shape, F32)

    def softmax_update(n_chunks, pv_fn):
        cm = s_ref[0]
        for g in range(1, n_chunks):
            cm = jnp.maximum(cm, s_ref[g])
        m_old = m_ref[...]
        m_new = jnp.maximum(m_old, jnp.max(cm, axis=1, keepdims=True))
        m_safe = jnp.where(m_new == NEG_INF, 0.0, m_new)
        alpha = jnp.exp(m_old - m_safe)
        psum = jnp.zeros((rows, PAGE), F32)
        pv = jnp.zeros((rows, D_ATT), F32)
        for g in range(n_chunks):
            p = jnp.exp(s_ref[g] - m_safe)
            psum = psum + p
            pv = pv + pv_fn(g, p.astype(BF16))
        l_ref[...] = l_ref[...] * alpha + jnp.sum(psum, axis=1, keepdims=True)
        acc_ref[...] = acc_ref[...] * alpha + pv
        m_ref[...] = m_new

    qblk = qblk_ref[...]
    for g in range(PG):
        pg = j * PG + g
        msk = jnp.concatenate([mask_ref[pg]] * N_HEADS, axis=0)
        jj = jnp.where(pg == n_pages - 1, 1, 0)
        s_ref[g] = (jnp.dot(qblk, k_refs[g][...].astype(BF16), preferred_element_type=F32)
                    + bias_ref[jj] + msk)
    softmax_update(PG, lambda g, p: _dot_nt(p, v_refs[g][...].astype(BF16)))

    @pl.when(j == pl.num_programs(1) - 1)
    def _():
        pad = jnp.zeros((PAGE - tn, D_ATT), BF16)
        kn = jnp.concatenate([kn_ref[...].astype(BF16), pad], axis=0)
        vn = jnp.concatenate([vn_ref[...].astype(BF16), pad], axis=0)
        msk = jnp.concatenate([mask_ref[n_pages]] * N_HEADS, axis=0)
        s_ref[0] = _dot_nt(qblk, kn) + bias_ref[2] + msk
        softmax_update(1, lambda g, p: jnp.dot(p, vn, preferred_element_type=F32))
        out = acc_ref[...] / l_ref[...]
        for h in range(N_HEADS):
            o_ref[:, h * HEAD_DIM:(h + 1) * HEAD_DIM] = out[h * tn:(h + 1) * tn,
                                                          h * HEAD_DIM:(h + 1) * HEAD_DIM]


def _sample_attn(page_tab, sp, ps, bias_s, ckT, cvT, layer, n_pages, blocks):
    db, tn, past = sp.shape
    npg = n_pages // PG
    rows = N_HEADS * tn
    topk = min(TOPK_MAX, (past + tn) // 4)
    pos_bits = (past + PAGE - 1).bit_length()
    body = functools.partial(_sattn_body, topk=topk, pos_bits=pos_bits, n_pages=n_pages)
    blk_q, blk_k, blk_v, blk_qi, blk_kw = blocks

    def page_spec(g):
        return pl.BlockSpec((None, None, D_ATT, PAGE),
                            lambda b, j, pt: (layer, pt[b * n_pages + j * PG + g], 0, 0))

    tok = lambda n, blk: pl.BlockSpec((tn, n), lambda b, j, pt: (b, blk))
    grid_spec = pltpu.PrefetchScalarGridSpec(
        num_scalar_prefetch=1,
        grid=(db, npg),
        in_specs=[pl.BlockSpec((None, tn, past), lambda b, j, pt: (b, 0, 0)),
                  tok(D_ATT, blk_q), tok(D_ATT, blk_k), tok(D_ATT, blk_v), tok(D_QI, blk_qi),
                  tok(LANES, blk_kw),
                  pl.BlockSpec(bias_s.shape, lambda b, j, pt: (0, 0, 0))]
                 + [page_spec(g) for g in range(PG)] * 2,
        out_specs=pl.BlockSpec((tn, D_ATT), lambda b, j, pt: (b, 0)),
        scratch_shapes=[
            pltpu.VMEM((n_pages + 1, tn, PAGE), I32),
            pltpu.VMEM((n_pages + 1, tn, PAGE), F32),
            pltpu.VMEM((rows, D_ATT), BF16),
            pltpu.VMEM((PG, rows, PAGE), F32),
            pltpu.VMEM((rows, 1), F32),
            pltpu.VMEM((rows, 1), F32),
            pltpu.VMEM((rows, D_ATT), F32),
        ],
    )
    return pl.pallas_call(
        body,
        grid_spec=grid_spec,
        out_shape=jax.ShapeDtypeStruct((db * tn, D_ATT), F32),
        compiler_params=_cparams(("parallel", "arbitrary")),
        name="sample_attn",
    )(page_tab, sp, ps, ps, ps, ps, ps, bias_s, *([ckT] * PG), *([cvT] * PG))


def _bias_table(rel_bias, dist):
    onehot = jax.nn.one_hot(_t5_bucket(dist), N_BUCKETS, dtype=F32)
    tab = jnp.einsum("...b,bh->h...", onehot, rel_bias, precision=lax.Precision.HIGHEST)
    return tab


def kernel(x_prompt, x_sample, cache_k, cache_v, cache_idx, state_pool, page_table, rel_bias,
           norm_g, w_in, b_gate, w_pool_group, pool_scale, w_up_attn, w_up_pool, w_out,
           final_norm_g):
    batch, seq, d = x_prompt.shape
    db, tn, _ = x_sample.shape
    depth = w_in.shape[0]
    n_pool = cache_k.shape[1]
    n_pages = page_table.shape[1]
    d_pool = pool_scale.shape[1]
    sizes = (D_ATT, D_ATT, D_ATT, D_ATT, D_QI, IDX_DIM, N_IDX_HEADS, d_pool, d_pool, d, d)
    offs = [int(v) for v in np.concatenate([[0], np.cumsum(sizes)])]
    (o_q, o_k, o_v, o_za, o_qi, o_ki, o_w, o_u, o_zp, o_ga, o_gp) = offs[:-1]
    assert w_in.shape[2] == offs[-1] and seq % TQ == 0 and seq % TM == 0
    assert n_pages % PG == 0 and n_pages % PGI == 0 and tn == SUBLANES
    assert D_ATT == d_pool == 4 * POOL_GROUP and d == 2 * D_ATT

    wT = jnp.swapaxes(w_in, 1, 2).astype(BF16)
    seg = lambda o, n: wT[:, o:o + n, :]
    zrows = lambda n: jnp.zeros((depth, n, d), BF16)
    q_scale = HEAD_DIM ** -0.5 * LOG2E
    w_q2 = (jnp.swapaxes(w_in[:, :, o_q:o_q + D_ATT], 1, 2) * q_scale).astype(BF16)
    w_row = jnp.swapaxes(jnp.concatenate([
        seg(o_k, D_ATT), seg(o_u, d_pool), seg(o_zp, d_pool), seg(o_ga, d), seg(o_gp, d),
        seg(o_ki, IDX_DIM), zrows(LANES - IDX_DIM)], axis=1), 1, 2)
    w_col = jnp.concatenate([
        w_q2, seg(o_k, D_ATT), seg(o_v, D_ATT), seg(o_za, D_ATT),
        seg(o_qi, D_QI) * IDX_DIM ** -0.5, seg(o_ki, IDX_DIM), seg(o_w, N_IDX_HEADS),
        zrows(16 - N_IDX_HEADS)], axis=1)
    s_sizes = (D_ATT, D_ATT, D_ATT, D_ATT, D_QI, d_pool, d_pool, d, d)
    s_offs = [int(v) for v in np.concatenate([[0], np.cumsum(s_sizes)])]
    (so_q, so_k, so_v, so_za, so_qi, so_u, so_zp, so_ga, so_gp, so_kw) = s_offs
    w_smp = jnp.concatenate([
        seg(o_q, D_ATT), seg(o_k, D_ATT), seg(o_v, D_ATT), seg(o_za, D_ATT), seg(o_qi, D_QI),
        seg(o_u, d_pool), seg(o_zp, d_pool), seg(o_ga, d), seg(o_gp, d),
        seg(o_ki, IDX_DIM), seg(o_w, N_IDX_HEADS), zrows(LANES - IDX_DIM - N_IDX_HEADS)], axis=1)
    n_smp = w_smp.shape[1]
    smp_blocks = 5
    assert n_smp % (smp_blocks * LANES) == 0
    wg = w_pool_group.astype(BF16)
    wua = w_up_attn.astype(BF16)
    wup = w_up_pool.astype(BF16)
    wo = w_out.astype(BF16)
    fg = final_norm_g.reshape(1, d)

    r = jnp.arange(CH, dtype=I32)[:, None]
    qq = jnp.arange(TQ, dtype=I32)[None, :]
    dist_p = jnp.stack([qq - r + CH * (jx + 1) - TQ for jx in range(NEAR)])
    bias_p = _bias_table(rel_bias, dist_p) * LOG2E
    bias_p = jnp.swapaxes(bias_p, 0, 1)
    cfar = jnp.broadcast_to((rel_bias[N_BUCKETS - 1] * LOG2E)[:, None], (N_HEADS, TQ))
    tt = jnp.arange(tn, dtype=I32)[:, None]
    ss = jnp.arange(PAGE, dtype=I32)[None, :]
    dist_s = jnp.stack([jnp.full((tn, PAGE), 2 * MAX_DISTANCE, I32), tt + PAGE - ss, tt - ss])
    bias_s = jnp.swapaxes(_bias_table(rel_bias, dist_s), 0, 1).reshape(3, N_HEADS * tn, PAGE)

    ckT = jnp.transpose(cache_k, (0, 1, 3, 4, 2)).reshape(depth, n_pool, D_ATT, PAGE)
    cvT = jnp.transpose(cache_v, (0, 1, 3, 4, 2)).reshape(depth, n_pool, D_ATT, PAGE)
    cidxT = jnp.swapaxes(cache_idx, 2, 3)
    page_flat = page_table.reshape(-1).astype(I32)

    xp = x_prompt.reshape(batch * seq, d)
    xs = x_sample.reshape(db * tn, d)
    kT_all = jnp.zeros((depth, batch, D_ATT, seq), F32)
    vT_all = jnp.zeros((depth, batch, D_ATT, seq), F32)
    kiT_all = jnp.zeros((depth, batch, IDX_DIM, seq), F32)
    pool_p, k_smp, v_smp, ki_smp, pool_s = [], [], [], [], []
    for l in range(depth):
        final = l == depth - 1
        g_l = norm_g[l].reshape(1, d)
        psc = pool_scale[l].reshape(1, d_pool)
        (kT_all, vT_all, kiT_all, u, zp, ga, gp, kb, kib, qT, vTc, zaT, qiT, wTi) = _inproj_prompt(
            xp, g_l, w_row[l], w_col[l], kT_all, vT_all, kiT_all, l, batch, seq, d_pool)
        aT = _attn_prompt(qT, qiT, wTi, kb, kib, vTc, bias_p, cfar, batch, seq)
        xp = _merge_prompt(xp, aT, zaT, u, zp, ga, gp, wg[l], psc, b_gate[l], wua[l], wup[l],
                           wo[l], fg, seq, final)
        pool_p.append(u.reshape(batch, seq, d_pool)[:, seq - POOL_HIST:])
        ps = _inproj_sample(xs, g_l, w_smp[l], smp_blocks)
        sp = _sample_idx_scores(page_flat, ps, cidxT, l, db, tn, n_pages,
                                so_qi // D_QI, so_kw // LANES)
        a_s = _sample_attn(page_flat, sp, ps, bias_s, ckT, cvT, l, n_pages,
                           (so_q // D_ATT, so_k // D_ATT, so_v // D_ATT, so_qi // D_QI,
                            so_kw // LANES))
        u_s = ps[:, so_u:so_u + d_pool].reshape(db, tn, d_pool)
        u_ext = jnp.concatenate([state_pool[l], u_s], axis=1)
        ue = jnp.pad(u_ext, ((0, 0), (HIST_PAD - POOL_HIST, 0), (0, 0)))
        xs = _merge_sample(xs, a_s, ps, ue, wg[l], psc, b_gate[l], wua[l], wup[l], wo[l], fg,
                           (so_za, so_zp, so_ga, so_gp, d), final)
        k_smp.append(ps[:, so_k:so_k + D_ATT].reshape(db, tn, N_HEADS, HEAD_DIM))
        v_smp.append(ps[:, so_v:so_v + D_ATT].reshape(db, tn, N_HEADS, HEAD_DIM))
        ki_smp.append(ps[:, so_kw:so_kw + IDX_DIM].reshape(db, tn, IDX_DIM))
        pool_s.append(u_ext[:, tn:])
    heads_last = lambda a: jnp.transpose(a.reshape(depth, batch, N_HEADS, HEAD_DIM, seq),
                                         (0, 1, 4, 2, 3))
    return (xp.reshape(batch, seq, d), xs.reshape(db, tn, d),
            heads_last(kT_all), heads_last(vT_all), jnp.swapaxes(kiT_all, 2, 3),
            jnp.stack(pool_p), jnp.stack(k_smp), jnp.stack(v_smp), jnp.stack(ki_smp),
            jnp.stack(pool_s))
```

```python
import functools

import numpy as np
import jax
import jax.numpy as jnp
from jax import lax
from jax.experimental import pallas as pl
from jax.experimental.pallas import tpu as pltpu

F32 = jnp.float32
BF16 = jnp.bfloat16
I32 = jnp.int32
I16 = jnp.int16
I16_MIN = -32768

N_HEADS = 8
HEAD_DIM = 64
D_ATT = N_HEADS * HEAD_DIM
N_IDX_HEADS = 8
IDX_DIM = 64
D_QI = N_IDX_HEADS * IDX_DIM
TOPK_MAX = 256
PAGE = 128
POOL_WINDOWS = (2, 4, 8, 16)
POOL_GROUP = 128
POOL_HIST = 15
N_BUCKETS = 32
MAX_DISTANCE = 128
RMS_EPS = 1e-6
LOG2E = 1.4426950408889634

LANES = 128
SUBLANES = 8
HALF_ROWS = 16
VMEM_LIMIT = 56 * 1024 * 1024

TQ = 256
CH = 128
SB = 256
TM = 256
NEAR = 2 + TQ // CH
PV_ROWS = HEAD_DIM + 16
PG = 8
PGI = 32
HIST_PAD = 16

NEG_INF = float("-inf")
INT_MIN = np.int32(-2 ** 31)
KEY_NEG_INF = np.int32(-0x7F800000)


def _sortable(x):
    b = lax.bitcast_convert_type(x, I32)
    return jnp.where(b < 0, INT_MIN - b, b)


def _sigmoid(x):
    return 1.0 / (1.0 + jnp.exp(-x))


def _silu(x):
    return x * _sigmoid(x)


def _rms(x, g):
    ms = jnp.mean(x * x, axis=-1, keepdims=True)
    return (x * lax.rsqrt(ms + RMS_EPS)) * g


def _t5_bucket(dist):
    n = jnp.maximum(dist, 0)
    max_exact = N_BUCKETS // 2
    nf = jnp.maximum(n, 1).astype(F32)
    large = max_exact + (jnp.log(nf / max_exact) / np.log(MAX_DISTANCE / max_exact)
                         * (N_BUCKETS - max_exact)).astype(I32)
    large = jnp.minimum(large, N_BUCKETS - 1)
    return jnp.where(n < max_exact, n, large)


def _dot_nt(a, b):
    return lax.dot_general(a, b, (((1,), (1,)), ((), ())), preferred_element_type=F32)


def _cparams(sem):
    return pltpu.CompilerParams(dimension_semantics=sem, vmem_limit_bytes=VMEM_LIMIT)


def _const_spec(shape):
    nd = len(shape)
    return pl.BlockSpec(shape, lambda *_: (0,) * nd)


def _tie_rule(count, thr, topk):
    need = topk - count(lambda kk, pos: kk > thr)
    tie_ok = thr > KEY_NEG_INF
    excess = jnp.logical_and(count(lambda kk, pos: kk == thr) > need, tie_ok)
    return need, tie_ok, jnp.max(jnp.where(excess, 1, 0)) > 0


def _last_tie_position(count, thr, need, pos_bits, shape):
    def pos_body(s, p):
        cand = p + jnp.left_shift(jnp.int32(1), pos_bits - 1 - s)
        cnt = count(lambda kk, pos: jnp.logical_and(kk == thr, pos < cand))
        return jnp.where(cnt <= need - 1, cand, p)
    return lax.fori_loop(0, pos_bits, pos_body, jnp.zeros(shape, I32))


def _tie_mask(kk, pos, thr, last, tie_ok):
    tie = jnp.logical_and(jnp.logical_and(kk == thr, pos <= last), tie_ok)
    sel = jnp.logical_and(jnp.logical_or(kk > thr, tie), kk > KEY_NEG_INF)
    return jnp.where(sel, 0.0, NEG_INF)


def _store_keys(key, r0, rows, key_ref, hi_ref, lo_ref):
    key_ref[pl.ds(r0, rows), :] = key
    hi_ref[pl.ds(r0, rows), :] = jnp.right_shift(key, 16).astype(I16)
    lo_ref[pl.ds(r0, rows), :] = ((key & 0xFFFF) + I16_MIN).astype(I16)


def _topk_mask(key_ref, hi_ref, lo_ref, y_ref, mask_ref, nsb, lanes, topk, pos_bits):
    row_sb = lax.broadcasted_iota(I32, (SB, lanes), 0)
    groups = SB // HALF_ROWS
    one16 = jnp.ones((groups, HALF_ROWS, lanes), I16)
    zero16 = jnp.zeros((groups, HALF_ROWS, lanes), I16)

    def count(pred):
        def body(c, acc):
            r0 = pl.multiple_of(c * SB, SB)
            hit = jnp.where(pred(key_ref[pl.ds(r0, SB), :], r0 + row_sb), 1, 0)
            return acc + hit.reshape(SB // SUBLANES, SUBLANES, lanes).sum(axis=0)
        acc = lax.fori_loop(0, nsb, body, jnp.zeros((SUBLANES, lanes), I32))
        return jnp.sum(acc, axis=0, keepdims=True)

    def fill_mask(fn):
        def body(c, carry):
            r0 = pl.multiple_of(c * SB, SB)
            mask_ref[pl.ds(r0, SB), :] = fn(key_ref[pl.ds(r0, SB), :], r0 + row_sb)
            return carry
        lax.fori_loop(0, nsb, body, 0)

    def count16(ref, pred):
        def body(c, acc):
            r0 = pl.multiple_of(c * SB, SB)
            blk = ref[pl.ds(r0, SB), :].reshape(groups, HALF_ROWS, lanes)
            hit = jnp.where(pred(blk), one16, zero16)
            parts = [hit[r] for r in range(groups)]
            while len(parts) > 1:
                parts = [a + b for a, b in zip(parts[0::2], parts[1::2])]
            return acc + parts[0]
        acc = lax.fori_loop(0, nsb, body, jnp.zeros((HALF_ROWS, lanes), I16))
        tot = jnp.sum(acc.astype(I32), axis=0, keepdims=True)
        return jnp.broadcast_to(tot, (HALF_ROWS, lanes))

    def select16(ref, target):
        def ge(cand32):
            c16 = cand32.astype(I16)[None]
            return count16(ref, lambda blk: blk >= c16)
        zeros = jnp.zeros((HALF_ROWS, lanes), I32)
        t0 = jnp.where(ge(zeros) >= target, zeros, jnp.full((HALF_ROWS, lanes), I16_MIN, I32))

        def bit_body(s, t):
            cand = t | jnp.left_shift(jnp.int32(1), 14 - s)
            return jnp.where(ge(cand) >= target, cand, t)
        return lax.fori_loop(0, 15, bit_body, t0)

    thr_hi = select16(hi_ref, topk)
    thr_hi16 = thr_hi.astype(I16)[None]
    need_lo = topk - count16(hi_ref, lambda blk: blk > thr_hi16)

    def y_body(c, carry):
        r0 = pl.multiple_of(c * SB, SB)
        hi = hi_ref[pl.ds(r0, SB), :].reshape(groups, HALF_ROWS, lanes)
        lo = lo_ref[pl.ds(r0, SB), :].reshape(groups, HALF_ROWS, lanes)
        y = jnp.where(hi == thr_hi16, lo, jnp.full(lo.shape, I16_MIN, I16))
        y_ref[pl.ds(r0, SB), :] = y.reshape(SB, lanes)
        return carry
    lax.fori_loop(0, nsb, y_body, 0)
    thr_lo = select16(y_ref, need_lo)
    thr = (thr_hi[0:1] * 65536) | (thr_lo[0:1] - I16_MIN)
    need, tie_ok, any_excess = _tie_rule(count, thr, topk)

    @pl.when(jnp.logical_not(any_excess))
    def _():
        thr_eff = jnp.maximum(thr, KEY_NEG_INF + 1)
        fill_mask(lambda kk, pos: jnp.where(kk >= thr_eff, 0.0, NEG_INF))

    @pl.when(any_excess)
    def _():
        last = _last_tie_position(count, thr, need, pos_bits, (1, lanes))
        fill_mask(lambda kk, pos: _tie_mask(kk, pos, thr, last, tie_ok))


def _inproj_prompt_body(x_ref, g_ref, wr_ref, wc_ref, kT_in, vT_in, kiT_in,
                        kT_ref, vT_ref, kiT_ref, u_ref, zp_ref, ga_ref, gp_ref,
                        kb_ref, kib_ref, qT_ref, vTc_ref, zaT_ref, qiT_ref, wT_ref):
    del kT_in, vT_in, kiT_in
    d_pool = u_ref.shape[1]
    d_model = ga_ref.shape[1]
    hb = _rms(x_ref[...], g_ref[...]).astype(BF16)

    def rowdot(a, n):
        return jnp.dot(hb, wr_ref[:, a:a + n], preferred_element_type=F32)

    def coldot(a, n):
        return _dot_nt(wc_ref[a:a + n, :], hb)

    o = 0
    kb_ref[...] = rowdot(o, D_ATT).astype(BF16); o += D_ATT
    u_ref[...] = rowdot(o, d_pool); o += d_pool
    zp_ref[...] = rowdot(o, d_pool); o += d_pool
    ga_ref[...] = rowdot(o, d_model); o += d_model
    gp_ref[...] = rowdot(o, d_model); o += d_model
    kib_ref[...] = rowdot(o, LANES)[:, :IDX_DIM].astype(BF16)

    o = 0
    qT_ref[...] = coldot(o, D_ATT).astype(BF16); o += D_ATT
    kT_ref[...] = coldot(o, D_ATT); o += D_ATT
    vT = coldot(o, D_ATT); o += D_ATT
    vT_ref[...] = vT
    vTb = vT.astype(BF16)
    for j in range(vTc_ref.shape[0]):
        vTc_ref[j] = vTb[:, j * LANES:(j + 1) * LANES]
    zaT_ref[...] = coldot(o, D_ATT); o += D_ATT
    qiT_ref[...] = coldot(o, D_QI).astype(BF16); o += D_QI
    kiT_ref[...] = coldot(o, IDX_DIM); o += IDX_DIM
    wT_ref[...] = coldot(o, 16)[0:N_IDX_HEADS] * (N_IDX_HEADS ** -0.5)


def _inproj_prompt(x, g, wr, wc, kT_all, vT_all, kiT_all, layer, batch, seq, d_pool):
    t, d = x.shape
    per = seq // TM
    row = lambda n: pl.BlockSpec((TM, n), lambda i: (i, 0))
    col = lambda n: pl.BlockSpec((n, TM), lambda i: (0, i))
    stacked = lambda n: pl.BlockSpec((None, None, n, TM), lambda i: (layer, i // per, 0, i % per))
    hbm = pl.BlockSpec(memory_space=pl.ANY)
    sds = jax.ShapeDtypeStruct
    out_shape = (
        sds(kT_all.shape, F32), sds(vT_all.shape, F32), sds(kiT_all.shape, F32),
        sds((t, d_pool), F32),
        sds((t, d_pool), F32),
        sds((t, d), F32),
        sds((t, d), F32),
        sds((t, D_ATT), BF16),
        sds((t, IDX_DIM), BF16),
        sds((D_ATT, t), BF16),
        sds((t // LANES, D_ATT, LANES), BF16),
        sds((D_ATT, t), F32),
        sds((D_QI, t), BF16),
        sds((N_IDX_HEADS, t), F32),
    )
    out_specs = (
        stacked(D_ATT), stacked(D_ATT), stacked(IDX_DIM),
        row(d_pool), row(d_pool), row(d), row(d), row(D_ATT), row(IDX_DIM), col(D_ATT),
        pl.BlockSpec((TM // LANES, D_ATT, LANES), lambda i: (i, 0, 0)),
        col(D_ATT), col(D_QI), col(N_IDX_HEADS),
    )
    return pl.pallas_call(
        _inproj_prompt_body,
        grid=(t // TM,),
        in_specs=[row(d), _const_spec((1, d)), _const_spec(wr.shape), _const_spec(wc.shape),
                  hbm, hbm, hbm],
        out_specs=out_specs,
        out_shape=out_shape,
        input_output_aliases={4: 0, 5: 1, 6: 2},
        compiler_params=_cparams(("parallel",)),
        name="inproj_prompt",
    )(x, g, wr, wc, kT_all, vT_all, kiT_all)


def _attn_prompt_body(qT_ref, qiT_ref, wT_ref, kb_ref, kib_ref, vT_ref, bias_ref, cfar_ref, o_ref,
                      key_ref, hi_ref, lo_ref, y_ref, mask_ref, qz_ref, s_ref, p_ref, m_ref,
                      alpha_ref, acc_ref,
                      *, topk, pos_bits):
    i = pl.program_id(1)
    nsb = i + 1
    nch = (TQ // CH) * (i + 1)

    row_k = lax.broadcasted_iota(I32, (CH, TQ), 0)
    lane_q = lax.broadcasted_iota(I32, (CH, TQ), 1) + i * TQ

    def idx_logits(c, buf):
        kc = kib_ref[pl.ds(pl.multiple_of(c * CH, CH), CH), :]
        for h in range(N_IDX_HEADS):
            s_ref[buf, h] = jnp.dot(kc, qiT_ref[h * IDX_DIM:(h + 1) * IDX_DIM, :],
                                    preferred_element_type=F32)

    def idx_keys(c, buf):
        r0 = pl.multiple_of(c * CH, CH)
        sc = jnp.zeros((CH, TQ), F32)
        for h in range(N_IDX_HEADS):
            sc = sc + wT_ref[h:h + 1, :] * jnp.maximum(s_ref[buf, h], 0.0)
        sc = jnp.where(r0 + row_k <= lane_q, sc, NEG_INF)
        _store_keys(_sortable(sc), r0, CH, key_ref, hi_ref, lo_ref)

    def idx_body(t, carry):
        idx_logits(2 * t, 0)
        idx_logits(2 * t + 1, 1)
        idx_keys(2 * t, 0)
        idx_keys(2 * t + 1, 1)
        return carry

    lax.fori_loop(0, nch // 2, idx_body, 0)

    _topk_mask(key_ref, hi_ref, lo_ref, y_ref, mask_ref, nsb, TQ, topk, pos_bits)

    zero_half = jnp.zeros((HEAD_DIM, TQ), BF16)
    for h in range(N_HEADS):
        own = qT_ref[h * HEAD_DIM:(h + 1) * HEAD_DIM, :]
        qz_ref[h] = jnp.concatenate([own, zero_half] if h % 2 == 0 else [zero_half, own], axis=0)
    m_ref[...] = jnp.full(m_ref.shape, NEG_INF, F32)
    acc_ref[...] = jnp.zeros(acc_ref.shape, F32)
    ones_rows = jnp.ones((PV_ROWS - HEAD_DIM, CH), BF16)

    def qk(c, buf):
        r0 = pl.multiple_of(c * CH, CH)
        for h in range(N_HEADS):
            pair = h // 2
            kp = kb_ref[pl.ds(r0, CH), pair * 2 * HEAD_DIM:(pair + 1) * 2 * HEAD_DIM]
            s_ref[buf, h] = jnp.dot(kp, qz_ref[h], preferred_element_type=F32)

    def softmax(c, buf, near):
        msk = mask_ref[pl.ds(pl.multiple_of(c * CH, CH), CH), :]
        for h in range(N_HEADS):
            if near is None:
                s = s_ref[buf, h] + msk
                off = cfar_ref[h:h + 1, :]
            else:
                s = s_ref[buf, h] + bias_ref[near, h] + msk
                off = 0.0
            m_old = m_ref[h:h + 1, :]
            m_new = jnp.maximum(m_old, jnp.max(s, axis=0, keepdims=True) + off)
            m_safe = jnp.where(m_new == NEG_INF, 0.0, m_new)
            alpha_ref[buf, h:h + 1, :] = jnp.exp2(m_old - m_safe)
            p_ref[buf, h] = jnp.exp2(s - (m_safe - off)).astype(BF16)
            m_ref[h:h + 1, :] = m_new

    def pv(c, buf):
        for h in range(N_HEADS):
            vt = jnp.concatenate([vT_ref[c, h * HEAD_DIM:(h + 1) * HEAD_DIM, :], ones_rows], axis=0)
            upd = jnp.dot(vt, p_ref[buf, h], preferred_element_type=F32)
            rows = slice(h * PV_ROWS, (h + 1) * PV_ROWS)
            acc_ref[rows, :] = acc_ref[rows, :] * alpha_ref[buf, h:h + 1, :] + upd

    def chunk_pair(c0, near0, near1):
        qk(c0, 0)
        qk(c0 + 1, 1)
        softmax(c0, 0, near0)
        pv(c0, 0)
        softmax(c0 + 1, 1, near1)
        pv(c0 + 1, 1)

    def far_body(t, carry):
        chunk_pair(2 * t, None, None)
        return carry

    lax.fori_loop(0, jnp.maximum((nch - NEAR) // 2, 0), far_body, 0)

    @pl.when(nch >= NEAR)
    def _():
        chunk_pair(nch - NEAR, NEAR - 1, NEAR - 2)

    chunk_pair(nch - 2, 1, 0)

    for h in range(N_HEADS):
        num = acc_ref[h * PV_ROWS:h * PV_ROWS + HEAD_DIM, :]
        den = acc_ref[h * PV_ROWS + HEAD_DIM:h * PV_ROWS + HEAD_DIM + 1, :]
        o_ref[h * HEAD_DIM:(h + 1) * HEAD_DIM, :] = num / den


def _attn_prompt(qT, qiT, wT, kb, kib, vTc, bias_tab, cfar, batch, seq):
    nq = seq // TQ
    topk = min(TOPK_MAX, seq // 4)
    pos_bits = max(1, (seq - 1).bit_length())
    body = functools.partial(_attn_prompt_body, topk=topk, pos_bits=pos_bits)
    qcol = lambda n: pl.BlockSpec((n, TQ), lambda b, i: (0, b * nq + i))
    return pl.pallas_call(
        body,
        grid=(batch, nq),
        in_specs=[
            qcol(D_ATT), qcol(D_QI), qcol(N_IDX_HEADS),
            pl.BlockSpec((seq, D_ATT), lambda b, i: (b, 0)),
            pl.BlockSpec((seq, IDX_DIM), lambda b, i: (b, 0)),
            pl.BlockSpec((seq // LANES, D_ATT, LANES), lambda b, i: (b, 0, 0)),
            _const_spec(bias_tab.shape), _const_spec(cfar.shape),
        ],
        out_specs=qcol(D_ATT),
        out_shape=jax.ShapeDtypeStruct((D_ATT, batch * seq), F32),
        scratch_shapes=[
            pltpu.VMEM((seq, TQ), I32),
            pltpu.VMEM((seq, TQ), I16),
            pltpu.VMEM((seq, TQ), I16),
            pltpu.VMEM((seq, TQ), I16),
            pltpu.VMEM((seq, TQ), F32),
            pltpu.VMEM((N_HEADS, 2 * HEAD_DIM, TQ), BF16),
            pltpu.VMEM((2, N_HEADS, CH, TQ), F32),
            pltpu.VMEM((2, N_HEADS, CH, TQ), BF16),
            pltpu.VMEM((N_HEADS, TQ), F32),
            pltpu.VMEM((2, N_HEADS, TQ), F32),
            pltpu.VMEM((N_HEADS * PV_ROWS, TQ), F32),
        ],
        compiler_params=_cparams(("parallel", "arbitrary")),
        name="attn_prompt",
    )(qT, qiT, wT, kb, kib, vTc, bias_tab, cfar)


def _merge_tail(x, a_z, p_out, zp, ga, gp, bg_ref, wua_ref, wup_ref, wo_ref):
    a = jnp.dot(a_z.astype(BF16), wua_ref[...], preferred_element_type=F32)
    p = jnp.dot((p_out * _silu(zp)).astype(BF16), wup_ref[...], preferred_element_type=F32)
    mix = _sigmoid(ga + bg_ref[0:1, :]) * a + _sigmoid(gp + bg_ref[1:2, :]) * p
    return x + jnp.dot(mix.astype(BF16), wo_ref[...], preferred_element_type=F32)


def _merge_prompt_body(x_ref, aT_ref, zaT_ref, u_ref, uh_ref, zp_ref, ga_ref, gp_ref,
                       wg_ref, ps_ref, bg_ref, wua_ref, wup_ref, wo_ref, fg_ref,
                       xo_ref, *rest, blocks_per_seq, final):
    ue_ref = rest[-1]
    i = pl.program_id(0)
    blk = i % blocks_per_seq
    ue_ref[0:HIST_PAD, :] = jnp.where(blk == 0, 0.0, uh_ref[...])
    ue_ref[HIST_PAD:HIST_PAD + TM, :] = u_ref[...]
    t_in_seq = blk * TM + lax.broadcasted_iota(I32, (TM, POOL_GROUP), 0)
    parts = []
    for g, w in enumerate(POOL_WINDOWS):
        cols = slice(g * POOL_GROUP, (g + 1) * POOL_GROUP)
        ssum = ue_ref[HIST_PAD:HIST_PAD + TM, cols]
        for j in range(1, w):
            ssum = ssum + ue_ref[HIST_PAD - j:HIST_PAD - j + TM, cols]
        cnt = jnp.minimum(t_in_seq + 1, w).astype(F32)
        diff = ssum / cnt - u_ref[:, cols]
        parts.append(jnp.dot(diff.astype(BF16), wg_ref[g], preferred_element_type=F32))
    p_out = jnp.concatenate(parts, axis=1) * ps_ref[...]
    a_z = (aT_ref[...] * _silu(zaT_ref[...])).T
    xn = _merge_tail(x_ref[...], a_z, p_out, zp_ref[...], ga_ref[...], gp_ref[...],
                     bg_ref, wua_ref, wup_ref, wo_ref)
    if final:
        xo_ref[...] = _rms(xn, fg_ref[...])
    else:
        xo_ref[...] = xn


def _merge_prompt(x, aT, zaT, u, zp, ga, gp, wg, ps, bg, wua, wup, wo, fg, seq, final):
    t, d = x.shape
    d_pool = u.shape[1]
    per = TM // HIST_PAD
    row = lambda n: pl.BlockSpec((TM, n), lambda i: (i, 0))
    col = lambda n: pl.BlockSpec((n, TM), lambda i: (0, i))
    body = functools.partial(_merge_prompt_body, blocks_per_seq=seq // TM, final=final)
    return pl.pallas_call(
        body,
        grid=(t // TM,),
        in_specs=[
            row(d), col(D_ATT), col(D_ATT), row(d_pool),
            pl.BlockSpec((HIST_PAD, d_pool), lambda i: (jnp.maximum(i * per - 1, 0), 0)),
            row(d_pool), row(d), row(d),
            _const_spec(wg.shape), _const_spec(ps.shape), _const_spec(bg.shape),
            _const_spec(wua.shape), _const_spec(wup.shape), _const_spec(wo.shape),
            _const_spec(fg.shape),
        ],
        out_specs=row(d),
        out_shape=jax.ShapeDtypeStruct((t, d), F32),
        scratch_shapes=[pltpu.VMEM((HIST_PAD + TM, d_pool), F32)],
        compiler_params=_cparams(("parallel",)),
        name="merge_prompt",
    )(x, aT, zaT, u, u, zp, ga, gp, wg, ps, bg, wua, wup, wo, fg)


def _merge_sample_body(x_ref, a_ref, ps_ref, ue_ref, wg_ref, psc_ref, bg_ref, wua_ref, wup_ref,
                       wo_ref, fg_ref, xo_ref, *, cols, final):
    db, ext, d_pool = ue_ref.shape
    tn = ext - HIST_PAD
    o_za, o_zp, o_ga, o_gp, d = cols
    parts = []
    for g, w in enumerate(POOL_WINDOWS):
        lanes = slice(g * POOL_GROUP, (g + 1) * POOL_GROUP)
        cur = ue_ref[:, HIST_PAD:HIST_PAD + tn, lanes]
        ssum = cur
        for j in range(1, w):
            ssum = ssum + ue_ref[:, HIST_PAD - j:HIST_PAD - j + tn, lanes]
        diff = (ssum / float(w) - cur).reshape(db * tn, POOL_GROUP)
        parts.append(jnp.dot(diff.astype(BF16), wg_ref[g], preferred_element_type=F32))
    p_out = jnp.concatenate(parts, axis=1) * psc_ref[...]
    a_z = a_ref[...] * _silu(ps_ref[:, o_za:o_za + D_ATT])
    xn = _merge_tail(x_ref[...], a_z, p_out, ps_ref[:, o_zp:o_zp + d_pool],
                     ps_ref[:, o_ga:o_ga + d], ps_ref[:, o_gp:o_gp + d],
                     bg_ref, wua_ref, wup_ref, wo_ref)
    if final:
        xo_ref[...] = _rms(xn, fg_ref[...])
    else:
        xo_ref[...] = xn


def _merge_sample(x, a, ps, ue, wg, psc, bg, wua, wup, wo, fg, cols, final):
    t, d = x.shape
    args = (x, a, ps, ue, wg, psc, bg, wua, wup, wo, fg)
    return pl.pallas_call(
        functools.partial(_merge_sample_body, cols=cols, final=final),
        grid=(1,),
        in_specs=[_const_spec(v.shape) for v in args],
        out_specs=_const_spec((t, d)),
        out_shape=jax.ShapeDtypeStruct((t, d), F32),
        compiler_params=_cparams(("arbitrary",)),
        name="merge_sample",
    )(*args)


def _inproj_sample_body(x_ref, g_ref, w_ref, o_ref):
    hb = _rms(x_ref[...], g_ref[...]).astype(BF16)
    o_ref[...] = _dot_nt(hb, w_ref[...])


def _inproj_sample(x, g, wT, nblk):
    t, d = x.shape
    n = wT.shape[0]
    bn = n // nblk
    return pl.pallas_call(
        _inproj_sample_body,
        grid=(nblk,),
        in_specs=[_const_spec((t, d)), _const_spec((1, d)), pl.BlockSpec((bn, d), lambda j: (j, 0))],
        out_specs=pl.BlockSpec((t, bn), lambda j: (0, j)),
        out_shape=jax.ShapeDtypeStruct((t, n), F32),
        compiler_params=_cparams(("parallel",)),
        name="inproj_sample",
    )(x, g, wT)


def _idx_lhs(qi_blk):
    parts = [qi_blk[:, h * IDX_DIM:(h + 1) * IDX_DIM] for h in range(N_IDX_HEADS)]
    return (jnp.concatenate(parts, axis=0) * IDX_DIM ** -0.5).astype(BF16)


def _idx_scores(lg, kw_blk):
    tn = lg.shape[0] // N_IDX_HEADS
    sc = jnp.zeros((tn, lg.shape[1]), F32)
    for h in range(N_IDX_HEADS):
        w = kw_blk[:, IDX_DIM + h:IDX_DIM + h + 1] * N_IDX_HEADS ** -0.5
        sc = sc + w * jnp.maximum(lg[h * tn:(h + 1) * tn, :], 0.0)
    return sc


def _sidx_body(pt_ref, qi_ref, kw_ref, *rest):
    c_refs, o_ref, onew_ref = rest[:PGI], rest[PGI], rest[PGI + 1]
    tn = o_ref.shape[0]
    qi = _idx_lhs(qi_ref[...])
    kw = kw_ref[...]
    for g in range(PGI):
        lg = jnp.dot(qi, c_refs[g][...].astype(BF16), preferred_element_type=F32)
        o_ref[:, g * PAGE:(g + 1) * PAGE] = _idx_scores(lg, kw)
    kin = jnp.concatenate([kw[:, :IDX_DIM].astype(BF16),
                           jnp.zeros((PAGE - tn, IDX_DIM), BF16)], axis=0)
    sc_new = _idx_scores(_dot_nt(qi, kin), kw)
    col = lax.broadcasted_iota(I32, (tn, PAGE), 1)
    row = lax.broadcasted_iota(I32, (tn, PAGE), 0)
    onew_ref[...] = jnp.where(col <= row, sc_new, NEG_INF)


def _sample_idx_scores(page_tab, ps, cidxT, layer, db, tn, n_pages, blk_qi, blk_kw):
    npg = n_pages // PGI

    def page_spec(g):
        return pl.BlockSpec((None, None, IDX_DIM, PAGE),
                            lambda b, j, pt: (layer, pt[b * n_pages + j * PGI + g], 0, 0))

    grid_spec = pltpu.PrefetchScalarGridSpec(
        num_scalar_prefetch=1,
        grid=(db, npg),
        in_specs=[pl.BlockSpec((tn, D_QI), lambda b, j, pt: (b, blk_qi)),
                  pl.BlockSpec((tn, LANES), lambda b, j, pt: (b, blk_kw))]
                 + [page_spec(g) for g in range(PGI)],
        out_specs=(pl.BlockSpec((None, tn, PGI * PAGE), lambda b, j, pt: (b, 0, j)),
                   pl.BlockSpec((None, tn, PAGE), lambda b, j, pt: (b, 0, 0))),
    )
    return pl.pallas_call(
        _sidx_body,
        grid_spec=grid_spec,
        out_shape=(jax.ShapeDtypeStruct((db, tn, n_pages * PAGE), F32),
                   jax.ShapeDtypeStruct((db, tn, PAGE), F32)),
        compiler_params=_cparams(("parallel", "arbitrary")),
        name="sample_idx",
    )(page_tab, ps, ps, *([cidxT] * PGI))


def _sselect_body(sp_ref, o_ref, key_ref, hi_ref, lo_ref, y_ref, mask_ref, *, topk, pos_bits, nsb):
    lanes, nk = sp_ref.shape
    for c in range(nk // PAGE):
        keyT = _sortable(sp_ref[:, c * PAGE:(c + 1) * PAGE].T)
        _store_keys(keyT, c * PAGE, PAGE, key_ref, hi_ref, lo_ref)
    pad = nsb * SB - nk
    if pad:
        _store_keys(jnp.full((pad, lanes), KEY_NEG_INF, I32), nk, pad, key_ref, hi_ref, lo_ref)
    _topk_mask(key_ref, hi_ref, lo_ref, y_ref, mask_ref, nsb, lanes, topk, pos_bits)
    for c in range(nk // PAGE):
        o_ref[:, c * PAGE:(c + 1) * PAGE] = mask_ref[c * PAGE:(c + 1) * PAGE, :].T


def _sample_select(sp2d, topk):
    nq, nk = sp2d.shape
    nsb = pl.cdiv(nk, SB)
    rows = nsb * SB
    body = functools.partial(_sselect_body, topk=topk, pos_bits=(nk - 1).bit_length(), nsb=nsb)
    spec = pl.BlockSpec((LANES, nk), lambda i: (i, 0))
    return pl.pallas_call(
        body,
        grid=(nq // LANES,),
        in_specs=[spec],
        out_specs=spec,
        out_shape=jax.ShapeDtypeStruct((nq, nk), F32),
        scratch_shapes=[pltpu.VMEM((rows, LANES), I32), pltpu.VMEM((rows, LANES), I16),
                        pltpu.VMEM((rows, LANES), I16), pltpu.VMEM((rows, LANES), I16),
                        pltpu.VMEM((rows, LANES), F32)],
        compiler_params=_cparams(("parallel",)),
        name="sample_select",
    )(sp2d)


def _sattn_body(pt_ref, mk_ref, q_ref, kn_ref, vn_ref, bias_ref, *rest, n_pages):
    k_refs, v_refs = rest[:PG], rest[PG:2 * PG]
    o_ref, mask_ref, qblk_ref, s_ref, m_ref, l_ref, acc_ref = rest[2 * PG:]
    j = pl.program_id(1)
    tn = o_ref.shape[0]
    rows = N_HEADS * tn

    @pl.when(j == 0)
    def _():
        for c in range(n_pages + 1):
            mask_ref[c] = mk_ref[:, c * PAGE:(c + 1) * PAGE]

        q8 = jnp.concatenate([q_ref[...] * HEAD_DIM ** -0.5] * N_HEADS, axis=0)
        rr = lax.broadcasted_iota(I32, (rows, D_ATT), 0) // tn
        cc = lax.broadcasted_iota(I32, (rows, D_ATT), 1) // HEAD_DIM
        qblk_ref[...] = jnp.where(rr == cc, q8, 0.0).astype(BF16)
        m_ref[...] = jnp.full(m_ref.shape, NEG_INF, F32)
        l_ref[...] = jnp.zeros(l_ref.shape, F32)
        acc_ref[...] = jnp.zeros(acc_ref.shape, F32)

    def softmax_update(n_chunks, pv_fn):
        cm = s_ref[0]
        for g in range(1, n_chunks):
            cm = jnp.maximum(cm, s_ref[g])
        m_old = m_ref[...]
        m_new = jnp.maximum(m_old, jnp.max(cm, axis=1, keepdims=True))
        m_safe = jnp.where(m_new == NEG_INF, 0.0, m_new)
        alpha = jnp.exp(m_old - m_safe)
        psum = jnp.zeros((rows, PAGE), F32)
        pv = jnp.zeros((rows, D_ATT), F32)
        for g in range(n_chunks):
            p = jnp.exp(s_ref[g] - m_safe)
            psum = psum + p
            pv = pv + pv_fn(g, p.astype(BF16))
        l_ref[...] = l_ref[...] * alpha + jnp.sum(psum, axis=1, keepdims=True)
        acc_ref[...] = acc_ref[...] * alpha + pv
        m_ref[...] = m_new

    qblk = qblk_ref[...]
    for g in range(PG):
        pg = j * PG + g
        msk = jnp.concatenate([mask_ref[pg]] * N_HEADS, axis=0)
        jj = jnp.where(pg == n_pages - 1, 1, 0)
        s_ref[g] = (jnp.dot(qblk, k_refs[g][...].astype(BF16), preferred_element_type=F32)
                    + bias_ref[jj] + msk)
    softmax_update(PG, lambda g, p: _dot_nt(p, v_refs[g][...].astype(BF16)))

    @pl.when(j == pl.num_programs(1) - 1)
    def _():
        pad = jnp.zeros((PAGE - tn, D_ATT), BF16)
        kn = jnp.concatenate([kn_ref[...].astype(BF16), pad], axis=0)
        vn = jnp.concatenate([vn_ref[...].astype(BF16), pad], axis=0)
        msk = jnp.concatenate([mask_ref[n_pages]] * N_HEADS, axis=0)
        s_ref[0] = _dot_nt(qblk, kn) + bias_ref[2] + msk
        softmax_update(1, lambda g, p: jnp.dot(p, vn, preferred_element_type=F32))
        out = acc_ref[...] / l_ref[...]
        for h in range(N_HEADS):
            o_ref[:, h * HEAD_DIM:(h + 1) * HEAD_DIM] = out[h * tn:(h + 1) * tn,
                                                          h * HEAD_DIM:(h + 1) * HEAD_DIM]


def _sample_attn(page_tab, mask2d, ps, bias_s, ckT, cvT, layer, n_pages, tn, blocks):
    nk = mask2d.shape[1]
    db = mask2d.shape[0] // tn
    npg = n_pages // PG
    rows = N_HEADS * tn
    body = functools.partial(_sattn_body, n_pages=n_pages)
    blk_q, blk_k, blk_v = blocks

    def page_spec(g):
        return pl.BlockSpec((None, None, D_ATT, PAGE),
                            lambda b, j, pt: (layer, pt[b * n_pages + j * PG + g], 0, 0))

    tok = lambda n, blk: pl.BlockSpec((tn, n), lambda b, j, pt: (b, blk))
    grid_spec = pltpu.PrefetchScalarGridSpec(
        num_scalar_prefetch=1,
        grid=(db, npg),
        in_specs=[tok(nk, 0), tok(D_ATT, blk_q), tok(D_ATT, blk_k), tok(D_ATT, blk_v),
                  pl.BlockSpec(bias_s.shape, lambda b, j, pt: (0, 0, 0))]
                 + [page_spec(g) for g in range(PG)] * 2,
        out_specs=pl.BlockSpec((tn, D_ATT), lambda b, j, pt: (b, 0)),
        scratch_shapes=[
            pltpu.VMEM((n_pages + 1, tn, PAGE), F32),
            pltpu.VMEM((rows, D_ATT), BF16),
            pltpu.VMEM((PG, rows, PAGE), F32),
            pltpu.VMEM((rows, 1), F32),
            pltpu.VMEM((rows, 1), F32),
            pltpu.VMEM((rows, D_ATT), F32),
        ],
    )
    return pl.pallas_call(
        body,
        grid_spec=grid_spec,
        out_shape=jax.ShapeDtypeStruct((db * tn, D_ATT), F32),
        compiler_params=_cparams(("parallel", "arbitrary")),
        name="sample_attn",
    )(page_tab, mask2d, ps, ps, ps, bias_s, *([ckT] * PG), *([cvT] * PG))


def _bias_table(rel_bias, dist):
    onehot = jax.nn.one_hot(_t5_bucket(dist), N_BUCKETS, dtype=F32)
    tab = jnp.einsum("...b,bh->h...", onehot, rel_bias, precision=lax.Precision.HIGHEST)
    return tab


def kernel(x_prompt, x_sample, cache_k, cache_v, cache_idx, state_pool, page_table, rel_bias,
           norm_g, w_in, b_gate, w_pool_group, pool_scale, w_up_attn, w_up_pool, w_out,
           final_norm_g):
    batch, seq, d = x_prompt.shape
    db, tn, _ = x_sample.shape
    depth = w_in.shape[0]
    n_pool = cache_k.shape[1]
    n_pages = page_table.shape[1]
    d_pool = pool_scale.shape[1]
    sizes = (D_ATT, D_ATT, D_ATT, D_ATT, D_QI, IDX_DIM, N_IDX_HEADS, d_pool, d_pool, d, d)
    offs = [int(v) for v in np.concatenate([[0], np.cumsum(sizes)])]
    (o_q, o_k, o_v, o_za, o_qi, o_ki, o_w, o_u, o_zp, o_ga, o_gp) = offs[:-1]
    assert w_in.shape[2] == offs[-1] and seq % TQ == 0 and seq % TM == 0
    assert n_pages % PG == 0 and n_pages % PGI == 0 and tn == SUBLANES and (db * tn) % LANES == 0
    assert D_ATT == d_pool == 4 * POOL_GROUP and d == 2 * D_ATT

    wT = jnp.swapaxes(w_in, 1, 2).astype(BF16)
    seg = lambda o, n: wT[:, o:o + n, :]
    zrows = lambda n: jnp.zeros((depth, n, d), BF16)
    q_scale = HEAD_DIM ** -0.5 * LOG2E
    w_q2 = (jnp.swapaxes(w_in[:, :, o_q:o_q + D_ATT], 1, 2) * q_scale).astype(BF16)
    w_row = jnp.swapaxes(jnp.concatenate([
        seg(o_k, D_ATT), seg(o_u, d_pool), seg(o_zp, d_pool), seg(o_ga, d), seg(o_gp, d),
        seg(o_ki, IDX_DIM), zrows(LANES - IDX_DIM)], axis=1), 1, 2)
    w_col = jnp.concatenate([
        w_q2, seg(o_k, D_ATT), seg(o_v, D_ATT), seg(o_za, D_ATT),
        seg(o_qi, D_QI) * IDX_DIM ** -0.5, seg(o_ki, IDX_DIM), seg(o_w, N_IDX_HEADS),
        zrows(16 - N_IDX_HEADS)], axis=1)
    s_sizes = (D_ATT, D_ATT, D_ATT, D_ATT, D_QI, d_pool, d_pool, d, d)
    s_offs = [int(v) for v in np.concatenate([[0], np.cumsum(s_sizes)])]
    (so_q, so_k, so_v, so_za, so_qi, so_u, so_zp, so_ga, so_gp, so_kw) = s_offs
    w_smp = jnp.concatenate([
        seg(o_q, D_ATT), seg(o_k, D_ATT), seg(o_v, D_ATT), seg(o_za, D_ATT), seg(o_qi, D_QI),
        seg(o_u, d_pool), seg(o_zp, d_pool), seg(o_ga, d), seg(o_gp, d),
        seg(o_ki, IDX_DIM), seg(o_w, N_IDX_HEADS), zrows(LANES - IDX_DIM - N_IDX_HEADS)], axis=1)
    n_smp = w_smp.shape[1]
    smp_blocks = 5
    assert n_smp % (smp_blocks * LANES) == 0
    wg = w_pool_group.astype(BF16)
    wua = w_up_attn.astype(BF16)
    wup = w_up_pool.astype(BF16)
    wo = w_out.astype(BF16)
    fg = final_norm_g.reshape(1, d)

    r = jnp.arange(CH, dtype=I32)[:, None]
    qq = jnp.arange(TQ, dtype=I32)[None, :]
    dist_p = jnp.stack([qq - r + CH * (jx + 1) - TQ for jx in range(NEAR)])
    bias_p = _bias_table(rel_bias, dist_p) * LOG2E
    bias_p = jnp.swapaxes(bias_p, 0, 1)
    cfar = jnp.broadcast_to((rel_bias[N_BUCKETS - 1] * LOG2E)[:, None], (N_HEADS, TQ))
    tt = jnp.arange(tn, dtype=I32)[:, None]
    ss = jnp.arange(PAGE, dtype=I32)[None, :]
    dist_s = jnp.stack([jnp.full((tn, PAGE), 2 * MAX_DISTANCE, I32), tt + PAGE - ss, tt - ss])
    bias_s = jnp.swapaxes(_bias_table(rel_bias, dist_s), 0, 1).reshape(3, N_HEADS * tn, PAGE)

    ckT = jnp.transpose(cache_k, (0, 1, 3, 4, 2)).reshape(depth, n_pool, D_ATT, PAGE)
    cvT = jnp.transpose(cache_v, (0, 1, 3, 4, 2)).reshape(depth, n_pool, D_ATT, PAGE)
    cidxT = jnp.swapaxes(cache_idx, 2, 3)
    page_flat = page_table.reshape(-1).astype(I32)

    xp = x_prompt.reshape(batch * seq, d)
    xs = x_sample.reshape(db * tn, d)
    kT_all = jnp.zeros((depth, batch, D_ATT, seq), F32)
    vT_all = jnp.zeros((depth, batch, D_ATT, seq), F32)
    kiT_all = jnp.zeros((depth, batch, IDX_DIM, seq), F32)
    pool_p, k_smp, v_smp, ki_smp, pool_s = [], [], [], [], []
    for l in range(depth):
        final = l == depth - 1
        g_l = norm_g[l].reshape(1, d)
        psc = pool_scale[l].reshape(1, d_pool)
        (kT_all, vT_all, kiT_all, u, zp, ga, gp, kb, kib, qT, vTc, zaT, qiT, wTi) = _inproj_prompt(
            xp, g_l, w_row[l], w_col[l], kT_all, vT_all, kiT_all, l, batch, seq, d_pool)
        aT = _attn_prompt(qT, qiT, wTi, kb, kib, vTc, bias_p, cfar, batch, seq)
        xp = _merge_prompt(xp, aT, zaT, u, zp, ga, gp, wg[l], psc, b_gate[l], wua[l], wup[l],
                           wo[l], fg, seq, final)
        pool_p.append(u.reshape(batch, seq, d_pool)[:, seq - POOL_HIST:])
        ps = _inproj_sample(xs, g_l, w_smp[l], smp_blocks)
        sp, sp_new = _sample_idx_scores(page_flat, ps, cidxT, l, db, tn, n_pages,
                                        so_qi // D_QI, so_kw // LANES)
        sp2d = jnp.concatenate([sp, sp_new], axis=2).reshape(db * tn, -1)
        mask2d = _sample_select(sp2d, min(TOPK_MAX, (n_pages * PAGE + tn) // 4))
        a_s = _sample_attn(page_flat, mask2d, ps, bias_s, ckT, cvT, l, n_pages, tn,
                           (so_q // D_ATT, so_k // D_ATT, so_v // D_ATT))
        u_s = ps[:, so_u:so_u + d_pool].reshape(db, tn, d_pool)
        u_ext = jnp.concatenate([state_pool[l], u_s], axis=1)
        ue = jnp.pad(u_ext, ((0, 0), (HIST_PAD - POOL_HIST, 0), (0, 0)))
        xs = _merge_sample(xs, a_s, ps, ue, wg[l], psc, b_gate[l], wua[l], wup[l], wo[l], fg,
                           (so_za, so_zp, so_ga, so_gp, d), final)
        k_smp.append(ps[:, so_k:so_k + D_ATT].reshape(db, tn, N_HEADS, HEAD_DIM))
        v_smp.append(ps[:, so_v:so_v + D_ATT].reshape(db, tn, N_HEADS, HEAD_DIM))
        ki_smp.append(ps[:, so_kw:so_kw + IDX_DIM].reshape(db, tn, IDX_DIM))
        pool_s.append(u_ext[:, tn:])
    heads_last = lambda a: jnp.transpose(a.reshape(depth, batch, N_HEADS, HEAD_DIM, seq),
                                         (0, 1, 4, 2, 3))
    return (xp.reshape(batch, seq, d), xs.reshape(db, tn, d),
            heads_last(kT_all), heads_last(vT_all), jnp.swapaxes(kiT_all, 2, 3),
            jnp.stack(pool_p), jnp.stack(k_smp), jnp.stack(v_smp), jnp.stack(ki_smp),
            jnp.stack(pool_s))
```

```python
import functools

import numpy as np
import jax
import jax.numpy as jnp
from jax import lax
from jax.experimental import pallas as pl
from jax.experimental.pallas import tpu as pltpu

F32 = jnp.float32
BF16 = jnp.bfloat16
I32 = jnp.int32
I16 = jnp.int16
I16_MIN = -32768

N_HEADS = 8
HEAD_DIM = 64
D_ATT = N_HEADS * HEAD_DIM
N_IDX_HEADS = 8
IDX_DIM = 64
D_QI = N_IDX_HEADS * IDX_DIM
TOPK_MAX = 256
PAGE = 128
POOL_WINDOWS = (2, 4, 8, 16)
POOL_GROUP = 128
POOL_HIST = 15
N_BUCKETS = 32
MAX_DISTANCE = 128
RMS_EPS = 1e-6
LOG2E = 1.4426950408889634

LANES = 128
SUBLANES = 8
HALF_ROWS = 16
VMEM_LIMIT = 56 * 1024 * 1024

TQ = 256
CH = 128
SB = 256
TM = 256
NEAR = 2 + TQ // CH
PV_ROWS = HEAD_DIM + 16
PG = 16
PGI = 32
HIST_PAD = 16

NEG_INF = float("-inf")
INT_MIN = np.int32(-2 ** 31)
KEY_NEG_INF = np.int32(-0x7F800000)


def _sortable(x):
    b = lax.bitcast_convert_type(x, I32)
    return jnp.where(b < 0, INT_MIN - b, b)


def _sigmoid(x):
    return 1.0 / (1.0 + jnp.exp(-x))


def _silu(x):
    return x * _sigmoid(x)


def _rms(x, g):
    ms = jnp.mean(x * x, axis=-1, keepdims=True)
    return (x * lax.rsqrt(ms + RMS_EPS)) * g


def _t5_bucket(dist):
    n = jnp.maximum(dist, 0)
    max_exact = N_BUCKETS // 2
    nf = jnp.maximum(n, 1).astype(F32)
    large = max_exact + (jnp.log(nf / max_exact) / np.log(MAX_DISTANCE / max_exact)
                         * (N_BUCKETS - max_exact)).astype(I32)
    large = jnp.minimum(large, N_BUCKETS - 1)
    return jnp.where(n < max_exact, n, large)


def _dot_nt(a, b):
    return lax.dot_general(a, b, (((1,), (1,)), ((), ())), preferred_element_type=F32)


def _cparams(sem):
    return pltpu.CompilerParams(dimension_semantics=sem, vmem_limit_bytes=VMEM_LIMIT)


def _const_spec(shape):
    nd = len(shape)
    return pl.BlockSpec(shape, lambda *_: (0,) * nd)


def _tie_rule(count, thr, topk):
    need = topk - count(lambda kk, pos: kk > thr)
    tie_ok = thr > KEY_NEG_INF
    excess = jnp.logical_and(count(lambda kk, pos: kk == thr) > need, tie_ok)
    return need, tie_ok, jnp.max(jnp.where(excess, 1, 0)) > 0


def _last_tie_position(count, thr, need, pos_bits, shape):
    def pos_body(s, p):
        cand = p + jnp.left_shift(jnp.int32(1), pos_bits - 1 - s)
        cnt = count(lambda kk, pos: jnp.logical_and(kk == thr, pos < cand))
        return jnp.where(cnt <= need - 1, cand, p)
    return lax.fori_loop(0, pos_bits, pos_body, jnp.zeros(shape, I32))


def _tie_mask(kk, pos, thr, last, tie_ok):
    tie = jnp.logical_and(jnp.logical_and(kk == thr, pos <= last), tie_ok)
    sel = jnp.logical_and(jnp.logical_or(kk > thr, tie), kk > KEY_NEG_INF)
    return jnp.where(sel, 0.0, NEG_INF)


def _store_keys(key, r0, rows, key_ref, hi_ref, lo_ref):
    key_ref[pl.ds(r0, rows), :] = key
    hi_ref[pl.ds(r0, rows), :] = jnp.right_shift(key, 16).astype(I16)
    lo_ref[pl.ds(r0, rows), :] = ((key & 0xFFFF) + I16_MIN).astype(I16)


def _topk_mask(key_ref, hi_ref, lo_ref, y_ref, mask_ref, nsb, lanes, topk, pos_bits):
    row_sb = lax.broadcasted_iota(I32, (SB, lanes), 0)
    groups = SB // HALF_ROWS
    one16 = jnp.ones((groups, HALF_ROWS, lanes), I16)
    zero16 = jnp.zeros((groups, HALF_ROWS, lanes), I16)

    def count(pred):
        def body(c, acc):
            r0 = pl.multiple_of(c * SB, SB)
            hit = jnp.where(pred(key_ref[pl.ds(r0, SB), :], r0 + row_sb), 1, 0)
            return acc + hit.reshape(SB // SUBLANES, SUBLANES, lanes).sum(axis=0)
        acc = lax.fori_loop(0, nsb, body, jnp.zeros((SUBLANES, lanes), I32))
        return jnp.sum(acc, axis=0, keepdims=True)

    def fill_mask(fn):
        def body(c, carry):
            r0 = pl.multiple_of(c * SB, SB)
            mask_ref[pl.ds(r0, SB), :] = fn(key_ref[pl.ds(r0, SB), :], r0 + row_sb)
            return carry
        lax.fori_loop(0, nsb, body, 0)

    def count16(ref, pred):
        def body(c, acc):
            r0 = pl.multiple_of(c * SB, SB)
            blk = ref[pl.ds(r0, SB), :].reshape(groups, HALF_ROWS, lanes)
            hit = jnp.where(pred(blk), one16, zero16)
            parts = [hit[r] for r in range(groups)]
            while len(parts) > 1:
                parts = [a + b for a, b in zip(parts[0::2], parts[1::2])]
            return acc + parts[0]
        acc = lax.fori_loop(0, nsb, body, jnp.zeros((HALF_ROWS, lanes), I16))
        tot = jnp.sum(acc.astype(I32), axis=0, keepdims=True)
        return jnp.broadcast_to(tot, (HALF_ROWS, lanes))

    def select16(ref, target):
        def ge(cand32):
            c16 = cand32.astype(I16)[None]
            return count16(ref, lambda blk: blk >= c16)
        zeros = jnp.zeros((HALF_ROWS, lanes), I32)
        t0 = jnp.where(ge(zeros) >= target, zeros, jnp.full((HALF_ROWS, lanes), I16_MIN, I32))

        def bit_body(s, t):
            cand = t | jnp.left_shift(jnp.int32(1), 14 - s)
            return jnp.where(ge(cand) >= target, cand, t)
        return lax.fori_loop(0, 15, bit_body, t0)

    thr_hi = select16(hi_ref, topk)
    thr_hi16 = thr_hi.astype(I16)[None]
    need_lo = topk - count16(hi_ref, lambda blk: blk > thr_hi16)

    def y_body(c, carry):
        r0 = pl.multiple_of(c * SB, SB)
        hi = hi_ref[pl.ds(r0, SB), :].reshape(groups, HALF_ROWS, lanes)
        lo = lo_ref[pl.ds(r0, SB), :].reshape(groups, HALF_ROWS, lanes)
        y = jnp.where(hi == thr_hi16, lo, jnp.full(lo.shape, I16_MIN, I16))
        y_ref[pl.ds(r0, SB), :] = y.reshape(SB, lanes)
        return carry
    lax.fori_loop(0, nsb, y_body, 0)
    thr_lo = select16(y_ref, need_lo)
    thr = (thr_hi[0:1] * 65536) | (thr_lo[0:1] - I16_MIN)
    need, tie_ok, any_excess = _tie_rule(count, thr, topk)

    @pl.when(jnp.logical_not(any_excess))
    def _():
        thr_eff = jnp.maximum(thr, KEY_NEG_INF + 1)
        fill_mask(lambda kk, pos: jnp.where(kk >= thr_eff, 0.0, NEG_INF))

    @pl.when(any_excess)
    def _():
        last = _last_tie_position(count, thr, need, pos_bits, (1, lanes))
        fill_mask(lambda kk, pos: _tie_mask(kk, pos, thr, last, tie_ok))


def _inproj_prompt_body(x_ref, g_ref, wr_ref, wc_ref, kT_in, vT_in, kiT_in,
                        kT_ref, vT_ref, kiT_ref, u_ref, zp_ref, ga_ref, gp_ref,
                        kb_ref, kib_ref, qT_ref, vTc_ref, zaT_ref, qiT_ref, wT_ref):
    del kT_in, vT_in, kiT_in
    d_pool = u_ref.shape[1]
    d_model = ga_ref.shape[1]
    hb = _rms(x_ref[...], g_ref[...]).astype(BF16)

    def rowdot(a, n):
        return jnp.dot(hb, wr_ref[:, a:a + n], preferred_element_type=F32)

    def coldot(a, n):
        return _dot_nt(wc_ref[a:a + n, :], hb)

    o = 0
    kb_ref[...] = rowdot(o, D_ATT).astype(BF16); o += D_ATT
    u_ref[...] = rowdot(o, d_pool); o += d_pool
    zp_ref[...] = rowdot(o, d_pool); o += d_pool
    ga_ref[...] = rowdot(o, d_model); o += d_model
    gp_ref[...] = rowdot(o, d_model); o += d_model
    kib_ref[...] = rowdot(o, LANES)[:, :IDX_DIM].astype(BF16)

    o = 0
    qT_ref[...] = coldot(o, D_ATT).astype(BF16); o += D_ATT
    kT_ref[...] = coldot(o, D_ATT); o += D_ATT
    vT = coldot(o, D_ATT); o += D_ATT
    vT_ref[...] = vT
    vTb = vT.astype(BF16)
    for j in range(vTc_ref.shape[0]):
        vTc_ref[j] = vTb[:, j * LANES:(j + 1) * LANES]
    zaT_ref[...] = coldot(o, D_ATT); o += D_ATT
    qiT_ref[...] = coldot(o, D_QI).astype(BF16); o += D_QI
    kiT_ref[...] = coldot(o, IDX_DIM); o += IDX_DIM
    wT_ref[...] = coldot(o, 16)[0:N_IDX_HEADS] * (N_IDX_HEADS ** -0.5)


def _inproj_prompt(x, g, wr, wc, kT_all, vT_all, kiT_all, layer, batch, seq, d_pool):
    t, d = x.shape
    per = seq // TM
    row = lambda n: pl.BlockSpec((TM, n), lambda i: (i, 0))
    col = lambda n: pl.BlockSpec((n, TM), lambda i: (0, i))
    stacked = lambda n: pl.BlockSpec((None, None, n, TM), lambda i: (layer, i // per, 0, i % per))
    hbm = pl.BlockSpec(memory_space=pl.ANY)
    sds = jax.ShapeDtypeStruct
    out_shape = (
        sds(kT_all.shape, F32), sds(vT_all.shape, F32), sds(kiT_all.shape, F32),
        sds((t, d_pool), F32),
        sds((t, d_pool), F32),
        sds((t, d), F32),
        sds((t, d), F32),
        sds((t, D_ATT), BF16),
        sds((t, IDX_DIM), BF16),
        sds((D_ATT, t), BF16),
        sds((t // LANES, D_ATT, LANES), BF16),
        sds((D_ATT, t), F32),
        sds((D_QI, t), BF16),
        sds((N_IDX_HEADS, t), F32),
    )
    out_specs = (
        stacked(D_ATT), stacked(D_ATT), stacked(IDX_DIM),
        row(d_pool), row(d_pool), row(d), row(d), row(D_ATT), row(IDX_DIM), col(D_ATT),
        pl.BlockSpec((TM // LANES, D_ATT, LANES), lambda i: (i, 0, 0)),
        col(D_ATT), col(D_QI), col(N_IDX_HEADS),
    )
    return pl.pallas_call(
        _inproj_prompt_body,
        grid=(t // TM,),
        in_specs=[row(d), _const_spec((1, d)), _const_spec(wr.shape), _const_spec(wc.shape),
                  hbm, hbm, hbm],
        out_specs=out_specs,
        out_shape=out_shape,
        input_output_aliases={4: 0, 5: 1, 6: 2},
        compiler_params=_cparams(("parallel",)),
        name="inproj_prompt",
    )(x, g, wr, wc, kT_all, vT_all, kiT_all)


def _attn_prompt_body(qT_ref, qiT_ref, wT_ref, kb_ref, kib_ref, vT_ref, bias_ref, cfar_ref, o_ref,
                      key_ref, hi_ref, lo_ref, y_ref, mask_ref, qz_ref, s_ref, p_ref, m_ref,
                      alpha_ref, acc_ref,
                      *, topk, pos_bits):
    i = pl.program_id(1)
    nsb = i + 1
    nch = (TQ // CH) * (i + 1)

    row_k = lax.broadcasted_iota(I32, (CH, TQ), 0)
    lane_q = lax.broadcasted_iota(I32, (CH, TQ), 1) + i * TQ

    def idx_logits(c, buf):
        kc = kib_ref[pl.ds(pl.multiple_of(c * CH, CH), CH), :]
        for h in range(N_IDX_HEADS):
            s_ref[buf, h] = jnp.dot(kc, qiT_ref[h * IDX_DIM:(h + 1) * IDX_DIM, :],
                                    preferred_element_type=F32)

    def idx_keys(c, buf):
        r0 = pl.multiple_of(c * CH, CH)
        sc = jnp.zeros((CH, TQ), F32)
        for h in range(N_IDX_HEADS):
            sc = sc + wT_ref[h:h + 1, :] * jnp.maximum(s_ref[buf, h], 0.0)
        sc = jnp.where(r0 + row_k <= lane_q, sc, NEG_INF)
        _store_keys(_sortable(sc), r0, CH, key_ref, hi_ref, lo_ref)

    def idx_body(t, carry):
        idx_logits(2 * t, 0)
        idx_logits(2 * t + 1, 1)
        idx_keys(2 * t, 0)
        idx_keys(2 * t + 1, 1)
        return carry

    lax.fori_loop(0, nch // 2, idx_body, 0)

    _topk_mask(key_ref, hi_ref, lo_ref, y_ref, mask_ref, nsb, TQ, topk, pos_bits)

    zero_half = jnp.zeros((HEAD_DIM, TQ), BF16)
    for h in range(N_HEADS):
        own = qT_ref[h * HEAD_DIM:(h + 1) * HEAD_DIM, :]
        qz_ref[h] = jnp.concatenate([own, zero_half] if h % 2 == 0 else [zero_half, own], axis=0)
    m_ref[...] = jnp.full(m_ref.shape, NEG_INF, F32)
    acc_ref[...] = jnp.zeros(acc_ref.shape, F32)
    ones_rows = jnp.ones((PV_ROWS - HEAD_DIM, CH), BF16)

    def qk(c, buf):
        r0 = pl.multiple_of(c * CH, CH)
        for h in range(N_HEADS):
            pair = h // 2
            kp = kb_ref[pl.ds(r0, CH), pair * 2 * HEAD_DIM:(pair + 1) * 2 * HEAD_DIM]
            s_ref[buf, h] = jnp.dot(kp, qz_ref[h], preferred_element_type=F32)

    def softmax(c, buf, near):
        msk = mask_ref[pl.ds(pl.multiple_of(c * CH, CH), CH), :]
        for h in range(N_HEADS):
            if near is None:
                s = s_ref[buf, h] + msk
                off = cfar_ref[h:h + 1, :]
            else:
                s = s_ref[buf, h] + bias_ref[near, h] + msk
                off = 0.0
            m_old = m_ref[h:h + 1, :]
            m_new = jnp.maximum(m_old, jnp.max(s, axis=0, keepdims=True) + off)
            m_safe = jnp.where(m_new == NEG_INF, 0.0, m_new)
            alpha_ref[buf, h:h + 1, :] = jnp.exp2(m_old - m_safe)
            p_ref[buf, h] = jnp.exp2(s - (m_safe - off)).astype(BF16)
            m_ref[h:h + 1, :] = m_new

    def pv(c, buf):
        for h in range(N_HEADS):
            vt = jnp.concatenate([vT_ref[c, h * HEAD_DIM:(h + 1) * HEAD_DIM, :], ones_rows], axis=0)
            upd = jnp.dot(vt, p_ref[buf, h], preferred_element_type=F32)
            rows = slice(h * PV_ROWS, (h + 1) * PV_ROWS)
            acc_ref[rows, :] = acc_ref[rows, :] * alpha_ref[buf, h:h + 1, :] + upd

    def chunk_pair(c0, near0, near1):
        qk(c0, 0)
        qk(c0 + 1, 1)
        softmax(c0, 0, near0)
        pv(c0, 0)
        softmax(c0 + 1, 1, near1)
        pv(c0 + 1, 1)

    def far_body(t, carry):
        chunk_pair(4 * t, None, None)
        chunk_pair(4 * t + 2, None, None)
        return carry

    far_pairs = jnp.maximum((nch - NEAR) // 2, 0)
    lax.fori_loop(0, far_pairs // 2, far_body, 0)

    @pl.when(far_pairs % 2 == 1)
    def _():
        chunk_pair(2 * (far_pairs - 1), None, None)

    @pl.when(nch >= NEAR)
    def _():
        chunk_pair(nch - NEAR, NEAR - 1, NEAR - 2)

    chunk_pair(nch - 2, 1, 0)

    for h in range(N_HEADS):
        num = acc_ref[h * PV_ROWS:h * PV_ROWS + HEAD_DIM, :]
        den = acc_ref[h * PV_ROWS + HEAD_DIM:h * PV_ROWS + HEAD_DIM + 1, :]
        o_ref[h * HEAD_DIM:(h + 1) * HEAD_DIM, :] = num / den


def _attn_prompt(qT, qiT, wT, kb, kib, vTc, bias_tab, cfar, batch, seq):
    nq = seq // TQ
    topk = min(TOPK_MAX, seq // 4)
    pos_bits = max(1, (seq - 1).bit_length())
    body = functools.partial(_attn_prompt_body, topk=topk, pos_bits=pos_bits)
    qcol = lambda n: pl.BlockSpec((n, TQ), lambda b, i: (0, b * nq + i))
    return pl.pallas_call(
        body,
        grid=(batch, nq),
        in_specs=[
            qcol(D_ATT), qcol(D_QI), qcol(N_IDX_HEADS),
            pl.BlockSpec((seq, D_ATT), lambda b, i: (b, 0)),
            pl.BlockSpec((seq, IDX_DIM), lambda b, i: (b, 0)),
            pl.BlockSpec((seq // LANES, D_ATT, LANES), lambda b, i: (b, 0, 0)),
            _const_spec(bias_tab.shape), _const_spec(cfar.shape),
        ],
        out_specs=qcol(D_ATT),
        out_shape=jax.ShapeDtypeStruct((D_ATT, batch * seq), F32),
        scratch_shapes=[
            pltpu.VMEM((seq, TQ), I32),
            pltpu.VMEM((seq, TQ), I16),
            pltpu.VMEM((seq, TQ), I16),
            pltpu.VMEM((seq, TQ), I16),
            pltpu.VMEM((seq, TQ), F32),
            pltpu.VMEM((N_HEADS, 2 * HEAD_DIM, TQ), BF16),
            pltpu.VMEM((2, N_HEADS, CH, TQ), F32),
            pltpu.VMEM((2, N_HEADS, CH, TQ), BF16),
            pltpu.VMEM((N_HEADS, TQ), F32),
            pltpu.VMEM((2, N_HEADS, TQ), F32),
            pltpu.VMEM((N_HEADS * PV_ROWS, TQ), F32),
        ],
        compiler_params=_cparams(("parallel", "arbitrary")),
        name="attn_prompt",
    )(qT, qiT, wT, kb, kib, vTc, bias_tab, cfar)


def _merge_tail(x, a_z, p_out, zp, ga, gp, bg_ref, wua_ref, wup_ref, wo_ref):
    a = jnp.dot(a_z.astype(BF16), wua_ref[...], preferred_element_type=F32)
    p = jnp.dot((p_out * _silu(zp)).astype(BF16), wup_ref[...], preferred_element_type=F32)
    mix = _sigmoid(ga + bg_ref[0:1, :]) * a + _sigmoid(gp + bg_ref[1:2, :]) * p
    return x + jnp.dot(mix.astype(BF16), wo_ref[...], preferred_element_type=F32)


def _merge_prompt_body(x_ref, aT_ref, zaT_ref, u_ref, uh_ref, zp_ref, ga_ref, gp_ref,
                       wg_ref, ps_ref, bg_ref, wua_ref, wup_ref, wo_ref, fg_ref,
                       xo_ref, *rest, blocks_per_seq, final):
    ue_ref = rest[-1]
    i = pl.program_id(0)
    blk = i % blocks_per_seq
    ue_ref[0:HIST_PAD, :] = jnp.where(blk == 0, 0.0, uh_ref[...])
    ue_ref[HIST_PAD:HIST_PAD + TM, :] = u_ref[...]
    t_in_seq = blk * TM + lax.broadcasted_iota(I32, (TM, POOL_GROUP), 0)
    parts = []
    for g, w in enumerate(POOL_WINDOWS):
        cols = slice(g * POOL_GROUP, (g + 1) * POOL_GROUP)
        ssum = ue_ref[HIST_PAD:HIST_PAD + TM, cols]
        for j in range(1, w):
            ssum = ssum + ue_ref[HIST_PAD - j:HIST_PAD - j + TM, cols]
        cnt = jnp.minimum(t_in_seq + 1, w).astype(F32)
        diff = ssum / cnt - u_ref[:, cols]
        parts.append(jnp.dot(diff.astype(BF16), wg_ref[g], preferred_element_type=F32))
    p_out = jnp.concatenate(parts, axis=1) * ps_ref[...]
    a_z = (aT_ref[...] * _silu(zaT_ref[...])).T
    xn = _merge_tail(x_ref[...], a_z, p_out, zp_ref[...], ga_ref[...], gp_ref[...],
                     bg_ref, wua_ref, wup_ref, wo_ref)
    if final:
        xo_ref[...] = _rms(xn, fg_ref[...])
    else:
        xo_ref[...] = xn


def _merge_prompt(x, aT, zaT, u, zp, ga, gp, wg, ps, bg, wua, wup, wo, fg, seq, final):
    t, d = x.shape
    d_pool = u.shape[1]
    per = TM // HIST_PAD
    row = lambda n: pl.BlockSpec((TM, n), lambda i: (i, 0))
    col = lambda n: pl.BlockSpec((n, TM), lambda i: (0, i))
    body = functools.partial(_merge_prompt_body, blocks_per_seq=seq // TM, final=final)
    return pl.pallas_call(
        body,
        grid=(t // TM,),
        in_specs=[
            row(d), col(D_ATT), col(D_ATT), row(d_pool),
            pl.BlockSpec((HIST_PAD, d_pool), lambda i: (jnp.maximum(i * per - 1, 0), 0)),
            row(d_pool), row(d), row(d),
            _const_spec(wg.shape), _const_spec(ps.shape), _const_spec(bg.shape),
            _const_spec(wua.shape), _const_spec(wup.shape), _const_spec(wo.shape),
            _const_spec(fg.shape),
        ],
        out_specs=row(d),
        out_shape=jax.ShapeDtypeStruct((t, d), F32),
        scratch_shapes=[pltpu.VMEM((HIST_PAD + TM, d_pool), F32)],
        compiler_params=_cparams(("parallel",)),
        name="merge_prompt",
    )(x, aT, zaT, u, u, zp, ga, gp, wg, ps, bg, wua, wup, wo, fg)


def _merge_sample_body(x_ref, a_ref, ps_ref, ue_ref, wg_ref, psc_ref, bg_ref, wua_ref, wup_ref,
                       wo_ref, fg_ref, xo_ref, *, cols, final):
    db, ext, d_pool = ue_ref.shape
    tn = ext - HIST_PAD
    o_za, o_zp, o_ga, o_gp, d = cols
    parts = []
    for g, w in enumerate(POOL_WINDOWS):
        lanes = slice(g * POOL_GROUP, (g + 1) * POOL_GROUP)
        cur = ue_ref[:, HIST_PAD:HIST_PAD + tn, lanes]
        ssum = cur
        for j in range(1, w):
            ssum = ssum + ue_ref[:, HIST_PAD - j:HIST_PAD - j + tn, lanes]
        diff = (ssum / float(w) - cur).reshape(db * tn, POOL_GROUP)
        parts.append(jnp.dot(diff.astype(BF16), wg_ref[g], preferred_element_type=F32))
    p_out = jnp.concatenate(parts, axis=1) * psc_ref[...]
    a_z = a_ref[...] * _silu(ps_ref[:, o_za:o_za + D_ATT])
    xn = _merge_tail(x_ref[...], a_z, p_out, ps_ref[:, o_zp:o_zp + d_pool],
                     ps_ref[:, o_ga:o_ga + d], ps_ref[:, o_gp:o_gp + d],
                     bg_ref, wua_ref, wup_ref, wo_ref)
    if final:
        xo_ref[...] = _rms(xn, fg_ref[...])
    else:
        xo_ref[...] = xn


def _merge_sample(x, a, ps, ue, wg, psc, bg, wua, wup, wo, fg, cols, final):
    t, d = x.shape
    args = (x, a, ps, ue, wg, psc, bg, wua, wup, wo, fg)
    return pl.pallas_call(
        functools.partial(_merge_sample_body, cols=cols, final=final),
        grid=(1,),
        in_specs=[_const_spec(v.shape) for v in args],
        out_specs=_const_spec((t, d)),
        out_shape=jax.ShapeDtypeStruct((t, d), F32),
        compiler_params=_cparams(("arbitrary",)),
        name="merge_sample",
    )(*args)


def _inproj_sample_body(x_ref, g_ref, w_ref, o_ref):
    hb = _rms(x_ref[...], g_ref[...]).astype(BF16)
    o_ref[...] = _dot_nt(hb, w_ref[...])


def _inproj_sample(x, g, wT, nblk):
    t, d = x.shape
    n = wT.shape[0]
    bn = n // nblk
    return pl.pallas_call(
        _inproj_sample_body,
        grid=(nblk,),
        in_specs=[_const_spec((t, d)), _const_spec((1, d)), pl.BlockSpec((bn, d), lambda j: (j, 0))],
        out_specs=pl.BlockSpec((t, bn), lambda j: (0, j)),
        out_shape=jax.ShapeDtypeStruct((t, n), F32),
        compiler_params=_cparams(("parallel",)),
        name="inproj_sample",
    )(x, g, wT)


def _idx_lhs(qi_blk):
    parts = [qi_blk[:, h * IDX_DIM:(h + 1) * IDX_DIM] for h in range(N_IDX_HEADS)]
    return (jnp.concatenate(parts, axis=0) * IDX_DIM ** -0.5).astype(BF16)


def _idx_scores(lg, kw_blk):
    tn = lg.shape[0] // N_IDX_HEADS
    sc = jnp.zeros((tn, lg.shape[1]), F32)
    for h in range(N_IDX_HEADS):
        w = kw_blk[:, IDX_DIM + h:IDX_DIM + h + 1] * N_IDX_HEADS ** -0.5
        sc = sc + w * jnp.maximum(lg[h * tn:(h + 1) * tn, :], 0.0)
    return sc


def _sidx_body(pt_ref, qi_ref, kw_ref, *rest):
    c_refs, o_ref, onew_ref = rest[:PGI], rest[PGI], rest[PGI + 1]
    tn = o_ref.shape[0]
    qi = _idx_lhs(qi_ref[...])
    kw = kw_ref[...]
    for g in range(PGI):
        lg = jnp.dot(qi, c_refs[g][...].astype(BF16), preferred_element_type=F32)
        o_ref[:, g * PAGE:(g + 1) * PAGE] = _idx_scores(lg, kw)
    kin = jnp.concatenate([kw[:, :IDX_DIM].astype(BF16),
                           jnp.zeros((PAGE - tn, IDX_DIM), BF16)], axis=0)
    sc_new = _idx_scores(_dot_nt(qi, kin), kw)
    col = lax.broadcasted_iota(I32, (tn, PAGE), 1)
    row = lax.broadcasted_iota(I32, (tn, PAGE), 0)
    onew_ref[...] = jnp.where(col <= row, sc_new, NEG_INF)


def _sample_idx_scores(page_tab, ps, cidxT, layer, db, tn, n_pages, blk_qi, blk_kw):
    npg = n_pages // PGI

    def page_spec(g):
        return pl.BlockSpec((None, None, IDX_DIM, PAGE),
                            lambda b, j, pt: (layer, pt[b * n_pages + j * PGI + g], 0, 0))

    grid_spec = pltpu.PrefetchScalarGridSpec(
        num_scalar_prefetch=1,
        grid=(db, npg),
        in_specs=[pl.BlockSpec((tn, D_QI), lambda b, j, pt: (b, blk_qi)),
                  pl.BlockSpec((tn, LANES), lambda b, j, pt: (b, blk_kw))]
                 + [page_spec(g) for g in range(PGI)],
        out_specs=(pl.BlockSpec((None, tn, PGI * PAGE), lambda b, j, pt: (b, 0, j)),
                   pl.BlockSpec((None, tn, PAGE), lambda b, j, pt: (b, 0, 0))),
    )
    return pl.pallas_call(
        _sidx_body,
        grid_spec=grid_spec,
        out_shape=(jax.ShapeDtypeStruct((db, tn, n_pages * PAGE), F32),
                   jax.ShapeDtypeStruct((db, tn, PAGE), F32)),
        compiler_params=_cparams(("parallel", "arbitrary")),
        name="sample_idx",
    )(page_tab, ps, ps, *([cidxT] * PGI))


def _sselect_body(sp_ref, o_ref, key_ref, hi_ref, lo_ref, y_ref, mask_ref, *, topk, pos_bits, nsb):
    lanes, nk = sp_ref.shape
    for c in range(nk // PAGE):
        keyT = _sortable(sp_ref[:, c * PAGE:(c + 1) * PAGE].T)
        _store_keys(keyT, c * PAGE, PAGE, key_ref, hi_ref, lo_ref)
    pad = nsb * SB - nk
    if pad:
        _store_keys(jnp.full((pad, lanes), KEY_NEG_INF, I32), nk, pad, key_ref, hi_ref, lo_ref)
    _topk_mask(key_ref, hi_ref, lo_ref, y_ref, mask_ref, nsb, lanes, topk, pos_bits)
    for c in range(nk // PAGE):
        o_ref[:, c * PAGE:(c + 1) * PAGE] = mask_ref[c * PAGE:(c + 1) * PAGE, :].T


def _sample_select(sp2d, topk):
    nq, nk = sp2d.shape
    nsb = pl.cdiv(nk, SB)
    rows = nsb * SB
    body = functools.partial(_sselect_body, topk=topk, pos_bits=(nk - 1).bit_length(), nsb=nsb)
    spec = pl.BlockSpec((LANES, nk), lambda i: (i, 0))
    return pl.pallas_call(
        body,
        grid=(nq // LANES,),
        in_specs=[spec],
        out_specs=spec,
        out_shape=jax.ShapeDtypeStruct((nq, nk), F32),
        scratch_shapes=[pltpu.VMEM((rows, LANES), I32), pltpu.VMEM((rows, LANES), I16),
                        pltpu.VMEM((rows, LANES), I16), pltpu.VMEM((rows, LANES), I16),
                        pltpu.VMEM((rows, LANES), F32)],
        compiler_params=_cparams(("parallel",)),
        name="sample_select",
    )(sp2d)


def _sattn_body(pt_ref, mk_ref, q_ref, kn_ref, vn_ref, bias_ref, *rest, n_pages):
    k_refs, v_refs = rest[:PG], rest[PG:2 * PG]
    o_ref, mask_ref, qblk_ref, s_ref, m_ref, l_ref, acc_ref = rest[2 * PG:]
    j = pl.program_id(1)
    tn = o_ref.shape[0]
    rows = N_HEADS * tn

    @pl.when(j == 0)
    def _():
        for c in range(n_pages + 1):
            mask_ref[c] = mk_ref[:, c * PAGE:(c + 1) * PAGE]

        q8 = jnp.concatenate([q_ref[...] * HEAD_DIM ** -0.5] * N_HEADS, axis=0)
        rr = lax.broadcasted_iota(I32, (rows, D_ATT), 0) // tn
        cc = lax.broadcasted_iota(I32, (rows, D_ATT), 1) // HEAD_DIM
        qblk_ref[...] = jnp.where(rr == cc, q8, 0.0).astype(BF16)
        m_ref[...] = jnp.full(m_ref.shape, NEG_INF, F32)
        l_ref[...] = jnp.zeros(l_ref.shape, F32)
        acc_ref[...] = jnp.zeros(acc_ref.shape, F32)

    def softmax_update(n_chunks, pv_fn):
        cm = s_ref[0]
        for g in range(1, n_chunks):
            cm = jnp.maximum(cm, s_ref[g])
        m_old = m_ref[...]
        m_new = jnp.maximum(m_old, jnp.max(cm, axis=1, keepdims=True))
        m_safe = jnp.where(m_new == NEG_INF, 0.0, m_new)
        alpha = jnp.exp(m_old - m_safe)
        psum = jnp.zeros((rows, PAGE), F32)
        pv = jnp.zeros((rows, D_ATT), F32)
        for g in range(n_chunks):
            p = jnp.exp(s_ref[g] - m_safe)
            psum = psum + p
            pv = pv + pv_fn(g, p.astype(BF16))
        l_ref[...] = l_ref[...] * alpha + jnp.sum(psum, axis=1, keepdims=True)
        acc_ref[...] = acc_ref[...] * alpha + pv
        m_ref[...] = m_new

    qblk = qblk_ref[...]
    for g in range(PG):
        pg = j * PG + g
        msk = jnp.concatenate([mask_ref[pg]] * N_HEADS, axis=0)
        jj = jnp.where(pg == n_pages - 1, 1, 0)
        s_ref[g] = (jnp.dot(qblk, k_refs[g][...].astype(BF16), preferred_element_type=F32)
                    + bias_ref[jj] + msk)
    softmax_update(PG, lambda g, p: _dot_nt(p, v_refs[g][...].astype(BF16)))

    @pl.when(j == pl.num_programs(1) - 1)
    def _():
        pad = jnp.zeros((PAGE - tn, D_ATT), BF16)
        kn = jnp.concatenate([kn_ref[...].astype(BF16), pad], axis=0)
        vn = jnp.concatenate([vn_ref[...].astype(BF16), pad], axis=0)
        msk = jnp.concatenate([mask_ref[n_pages]] * N_HEADS, axis=0)
        s_ref[0] = _dot_nt(qblk, kn) + bias_ref[2] + msk
        softmax_update(1, lambda g, p: jnp.dot(p, vn, preferred_element_type=F32))
        out = acc_ref[...] / l_ref[...]
        for h in range(N_HEADS):
            o_ref[:, h * HEAD_DIM:(h + 1) * HEAD_DIM] = out[h * tn:(h + 1) * tn,
                                                          h * HEAD_DIM:(h + 1) * HEAD_DIM]


def _sample_attn(page_tab, mask2d, ps, bias_s, ckT, cvT, layer, n_pages, tn, blocks):
    nk = mask2d.shape[1]
    db = mask2d.shape[0] // tn
    npg = n_pages // PG
    rows = N_HEADS * tn
    body = functools.partial(_sattn_body, n_pages=n_pages)
    blk_q, blk_k, blk_v = blocks

    def page_spec(g):
        return pl.BlockSpec((None, None, D_ATT, PAGE),
                            lambda b, j, pt: (layer, pt[b * n_pages + j * PG + g], 0, 0))

    tok = lambda n, blk: pl.BlockSpec((tn, n), lambda b, j, pt: (b, blk))
    grid_spec = pltpu.PrefetchScalarGridSpec(
        num_scalar_prefetch=1,
        grid=(db, npg),
        in_specs=[tok(nk, 0), tok(D_ATT, blk_q), tok(D_ATT, blk_k), tok(D_ATT, blk_v),
                  pl.BlockSpec(bias_s.shape, lambda b, j, pt: (0, 0, 0))]
                 + [page_spec(g) for g in range(PG)] * 2,
        out_specs=pl.BlockSpec((tn, D_ATT), lambda b, j, pt: (b, 0)),
        scratch_shapes=[
            pltpu.VMEM((n_pages + 1, tn, PAGE), F32),
            pltpu.VMEM((rows, D_ATT), BF16),
            pltpu.VMEM((PG, rows, PAGE), F32),
            pltpu.VMEM((rows, 1), F32),
            pltpu.VMEM((rows, 1), F32),
            pltpu.VMEM((rows, D_ATT), F32),
        ],
    )
    return pl.pallas_call(
        body,
        grid_spec=grid_spec,
        out_shape=jax.ShapeDtypeStruct((db * tn, D_ATT), F32),
        compiler_params=_cparams(("parallel", "arbitrary")),
        name="sample_attn",
    )(page_tab, mask2d, ps, ps, ps, bias_s, *([ckT] * PG), *([cvT] * PG))


def _bias_table(rel_bias, dist):
    onehot = jax.nn.one_hot(_t5_bucket(dist), N_BUCKETS, dtype=F32)
    tab = jnp.einsum("...b,bh->h...", onehot, rel_bias, precision=lax.Precision.HIGHEST)
    return tab


def kernel(x_prompt, x_sample, cache_k, cache_v, cache_idx, state_pool, page_table, rel_bias,
           norm_g, w_in, b_gate, w_pool_group, pool_scale, w_up_attn, w_up_pool, w_out,
           final_norm_g):
    batch, seq, d = x_prompt.shape
    db, tn, _ = x_sample.shape
    depth = w_in.shape[0]
    n_pool = cache_k.shape[1]
    n_pages = page_table.shape[1]
    d_pool = pool_scale.shape[1]
    sizes = (D_ATT, D_ATT, D_ATT, D_ATT, D_QI, IDX_DIM, N_IDX_HEADS, d_pool, d_pool, d, d)
    offs = [int(v) for v in np.concatenate([[0], np.cumsum(sizes)])]
    (o_q, o_k, o_v, o_za, o_qi, o_ki, o_w, o_u, o_zp, o_ga, o_gp) = offs[:-1]
    assert w_in.shape[2] == offs[-1] and seq % TQ == 0 and seq % TM == 0
    assert n_pages % PG == 0 and n_pages % PGI == 0 and tn == SUBLANES and (db * tn) % LANES == 0
    assert D_ATT == d_pool == 4 * POOL_GROUP and d == 2 * D_ATT

    wT = jnp.swapaxes(w_in, 1, 2).astype(BF16)
    seg = lambda o, n: wT[:, o:o + n, :]
    zrows = lambda n: jnp.zeros((depth, n, d), BF16)
    q_scale = HEAD_DIM ** -0.5 * LOG2E
    w_q2 = (jnp.swapaxes(w_in[:, :, o_q:o_q + D_ATT], 1, 2) * q_scale).astype(BF16)
    w_row = jnp.swapaxes(jnp.concatenate([
        seg(o_k, D_ATT), seg(o_u, d_pool), seg(o_zp, d_pool), seg(o_ga, d), seg(o_gp, d),
        seg(o_ki, IDX_DIM), zrows(LANES - IDX_DIM)], axis=1), 1, 2)
    w_col = jnp.concatenate([
        w_q2, seg(o_k, D_ATT), seg(o_v, D_ATT), seg(o_za, D_ATT),
        seg(o_qi, D_QI) * IDX_DIM ** -0.5, seg(o_ki, IDX_DIM), seg(o_w, N_IDX_HEADS),
        zrows(16 - N_IDX_HEADS)], axis=1)
    s_sizes = (D_ATT, D_ATT, D_ATT, D_ATT, D_QI, d_pool, d_pool, d, d)
    s_offs = [int(v) for v in np.concatenate([[0], np.cumsum(s_sizes)])]
    (so_q, so_k, so_v, so_za, so_qi, so_u, so_zp, so_ga, so_gp, so_kw) = s_offs
    w_smp = jnp.concatenate([
        seg(o_q, D_ATT), seg(o_k, D_ATT), seg(o_v, D_ATT), seg(o_za, D_ATT), seg(o_qi, D_QI),
        seg(o_u, d_pool), seg(o_zp, d_pool), seg(o_ga, d), seg(o_gp, d),
        seg(o_ki, IDX_DIM), seg(o_w, N_IDX_HEADS), zrows(LANES - IDX_DIM - N_IDX_HEADS)], axis=1)
    n_smp = w_smp.shape[1]
    smp_blocks = 5
    assert n_smp % (smp_blocks * LANES) == 0
    wg = w_pool_group.astype(BF16)
    wua = w_up_attn.astype(BF16)
    wup = w_up_pool.astype(BF16)
    wo = w_out.astype(BF16)
    fg = final_norm_g.reshape(1, d)

    r = jnp.arange(CH, dtype=I32)[:, None]
    qq = jnp.arange(TQ, dtype=I32)[None, :]
    dist_p = jnp.stack([qq - r + CH * (jx + 1) - TQ for jx in range(NEAR)])
    bias_p = _bias_table(rel_bias, dist_p) * LOG2E
    bias_p = jnp.swapaxes(bias_p, 0, 1)
    cfar = jnp.broadcast_to((rel_bias[N_BUCKETS - 1] * LOG2E)[:, None], (N_HEADS, TQ))
    tt = jnp.arange(tn, dtype=I32)[:, None]
    ss = jnp.arange(PAGE, dtype=I32)[None, :]
    dist_s = jnp.stack([jnp.full((tn, PAGE), 2 * MAX_DISTANCE, I32), tt + PAGE - ss, tt - ss])
    bias_s = jnp.swapaxes(_bias_table(rel_bias, dist_s), 0, 1).reshape(3, N_HEADS * tn, PAGE)

    ckT = jnp.transpose(cache_k, (0, 1, 3, 4, 2)).reshape(depth, n_pool, D_ATT, PAGE)
    cvT = jnp.transpose(cache_v, (0, 1, 3, 4, 2)).reshape(depth, n_pool, D_ATT, PAGE)
    cidxT = jnp.swapaxes(cache_idx, 2, 3)
    page_flat = page_table.reshape(-1).astype(I32)

    xp = x_prompt.reshape(batch * seq, d)
    xs = x_sample.reshape(db * tn, d)
    kT_all = jnp.zeros((depth, batch, D_ATT, seq), F32)
    vT_all = jnp.zeros((depth, batch, D_ATT, seq), F32)
    kiT_all = jnp.zeros((depth, batch, IDX_DIM, seq), F32)
    pool_p, k_smp, v_smp, ki_smp, pool_s = [], [], [], [], []
    for l in range(depth):
        final = l == depth - 1
        g_l = norm_g[l].reshape(1, d)
        psc = pool_scale[l].reshape(1, d_pool)
        (kT_all, vT_all, kiT_all, u, zp, ga, gp, kb, kib, qT, vTc, zaT, qiT, wTi) = _inproj_prompt(
            xp, g_l, w_row[l], w_col[l], kT_all, vT_all, kiT_all, l, batch, seq, d_pool)
        aT = _attn_prompt(qT, qiT, wTi, kb, kib, vTc, bias_p, cfar, batch, seq)
        xp = _merge_prompt(xp, aT, zaT, u, zp, ga, gp, wg[l], psc, b_gate[l], wua[l], wup[l],
                           wo[l], fg, seq, final)
        pool_p.append(u.reshape(batch, seq, d_pool)[:, seq - POOL_HIST:])
        ps = _inproj_sample(xs, g_l, w_smp[l], smp_blocks)
        sp, sp_new = _sample_idx_scores(page_flat, ps, cidxT, l, db, tn, n_pages,
                                        so_qi // D_QI, so_kw // LANES)
        sp2d = jnp.concatenate([sp, sp_new], axis=2).reshape(db * tn, -1)
        mask2d = _sample_select(sp2d, min(TOPK_MAX, (n_pages * PAGE + tn) // 4))
        a_s = _sample_attn(page_flat, mask2d, ps, bias_s, ckT, cvT, l, n_pages, tn,
                           (so_q // D_ATT, so_k // D_ATT, so_v // D_ATT))
        u_s = ps[:, so_u:so_u + d_pool].reshape(db, tn, d_pool)
        u_ext = jnp.concatenate([state_pool[l], u_s], axis=1)
        ue = jnp.pad(u_ext, ((0, 0), (HIST_PAD - POOL_HIST, 0), (0, 0)))
        xs = _merge_sample(xs, a_s, ps, ue, wg[l], psc, b_gate[l], wua[l], wup[l], wo[l], fg,
                           (so_za, so_zp, so_ga, so_gp, d), final)
        k_smp.append(ps[:, so_k:so_k + D_ATT].reshape(db, tn, N_HEADS, HEAD_DIM))
        v_smp.append(ps[:, so_v:so_v + D_ATT].reshape(db, tn, N_HEADS, HEAD_DIM))
        ki_smp.append(ps[:, so_kw:so_kw + IDX_DIM].reshape(db, tn, IDX_DIM))
        pool_s.append(u_ext[:, tn:])
    heads_last = lambda a: jnp.transpose(a.reshape(depth, batch, N_HEADS, HEAD_DIM, seq),
                                         (0, 1, 4, 2, 3))
    return (xp.reshape(batch, seq, d), xs.reshape(db, tn, d),
            heads_last(kT_all), heads_last(vT_all), jnp.swapaxes(kiT_all, 2, 3),
            jnp.stack(pool_p), jnp.stack(k_smp), jnp.stack(v_smp), jnp.stack(ki_smp),
            jnp.stack(pool_s))
```
